```python
import math, functools
import jax, jax.numpy as jnp
from jax import lax
import numpy as np

D_MODEL = 1024
BATCH = 8
SEQ = 2048
DEPTH = 2
DEC_BATCH = 32
DEC_SEQ = 8
PAST_LEN = 16384
PAGE_SIZE = 128

HEAD_DIM = 64
CONV_W = 256
CONV_K = 3
MOBA_H = 6
FOX_H = 6
MOBA_W = MOBA_H * HEAD_DIM
FOX_W = FOX_H * HEAD_DIM
MIX_W = CONV_W + MOBA_W + FOX_W
IN_W = 3 * CONV_W + 3 * MOBA_W + 3 * FOX_W + FOX_H
MOBA_BLOCK = 256
MOBA_TOPK = 3
MOBA_Q_BLOCK = 64
FOX_Q_BLOCK = 128
ROPE_THETA = 10000.0
FORGET_BIAS_INIT = 2.0
D_FF = 3584
N_EXPERTS = 8
TOP_K = 2
EXPERT_D_FF = 3584
N_DENSE = (DEPTH + 1) // 2
N_MOE = DEPTH // 2
ALPHA = (2.0 * DEPTH) ** 0.25
BETA = (8.0 * DEPTH) ** -0.25
LN_EPS = 1e-5

kernel_name = 'hybrid_conv_moba_fox_decoder_step'


def layer_norm(x, g, b):
    xf = x.astype(jnp.float32)
    mu = xf.mean(-1, keepdims=True)
    var = jnp.square(xf - mu).mean(-1, keepdims=True)
    return ((xf - mu) * lax.rsqrt(var + LN_EPS) * g + b).astype(x.dtype)


def rope(x, pos0):
    t = x.shape[1]
    pos = (pos0 + jnp.arange(t)).astype(jnp.float32)
    inv = jnp.power(ROPE_THETA, -jnp.arange(0, HEAD_DIM, 2, dtype=jnp.float32) / HEAD_DIM)
    ang = pos[:, None] * inv[None, :]
    cos = jnp.cos(ang)[None, :, None, :]
    sin = jnp.sin(ang)[None, :, None, :]
    x1, x2 = jnp.split(x.astype(jnp.float32), 2, axis=-1)
    return jnp.concatenate([x1 * cos - x2 * sin, x2 * cos + x1 * sin], axis=-1).astype(x.dtype)


def _q_block(t, qb):
    return qb if t % qb == 0 else t


def moba_attention(q, k_all, v_all, pos0):
    b, t, h, d = q.shape
    nb = k_all.shape[1] // MOBA_BLOCK
    kb = k_all.reshape(b, nb, MOBA_BLOCK, h, d)
    vb = v_all.reshape(b, nb, MOBA_BLOCK, h, d)
    k_mean = kb.astype(jnp.float32).mean(axis=2)
    n_sel = min(MOBA_TOPK, nb)
    qb = _q_block(t, MOBA_Q_BLOCK)
    scale = HEAD_DIM ** -0.5
    b_ix = jnp.arange(b)[:, None, None, None]
    h_ix = jnp.arange(h)[None, None, :, None]
    blk = jnp.arange(nb)
    offs = jnp.arange(MOBA_BLOCK)

    def one_block(args):
        qc, pos = args
        own = pos // MOBA_BLOCK
        gate = jnp.einsum('bqhd,bnhd->bqhn', qc, k_mean, preferred_element_type=jnp.float32)
        fully_past = blk[None, :] < own[:, None]
        gate = jnp.where(fully_past[None, :, None, :], gate, -jnp.inf)
        _, top_i = lax.top_k(gate, n_sel)
        own_b = jnp.broadcast_to(own[None, :, None, None], (b, qb, h, 1))
        idx = jnp.concatenate([top_i, own_b], axis=-1)
        blk_ok = jnp.concatenate([top_i < own[None, :, None, None], jnp.ones((b, qb, h, 1), bool)], axis=-1)
        k_sel = kb[b_ix, idx, :, h_ix]
        v_sel = vb[b_ix, idx, :, h_ix]
        logits = jnp.einsum('bqhd,bqhjmd->bqhjm', qc, k_sel, preferred_element_type=jnp.float32) * scale
        key_pos = idx[..., None] * MOBA_BLOCK + offs
        mask = blk_ok[..., None] & (key_pos <= pos[None, :, None, None, None])
        logits = jnp.where(mask, logits, -jnp.inf).reshape(b, qb, h, -1)
        p = jax.nn.softmax(logits, axis=-1).reshape(b, qb, h, n_sel + 1, MOBA_BLOCK)
        return jnp.einsum('bqhjm,bqhjmd->bqhd', p, v_sel.astype(jnp.float32)).astype(qc.dtype)

    q_blocks = q.reshape(b, t // qb, qb, h, d).transpose(1, 0, 2, 3, 4)
    pos_blocks = (pos0 + jnp.arange(t)).reshape(t // qb, qb)
    out = lax.map(one_block, (q_blocks, pos_blocks))
    return out.transpose(1, 0, 2, 3, 4).reshape(b, t, h, d)


def forgetting_attention(q, k_all, v_all, logf_all, pos0):
    b, t, h, d = q.shape
    n_keys = k_all.shape[1]
    c = jnp.cumsum(logf_all.astype(jnp.float32), axis=1)
    c_k = c.transpose(0, 2, 1)
    c_q = c[:, pos0:pos0 + t]
    qb = _q_block(t, FOX_Q_BLOCK)
    scale = HEAD_DIM ** -0.5
    key_pos = jnp.arange(n_keys)

    def one_block(args):
        qc, cq, pos = args
        logits = jnp.einsum('bqhd,bkhd->bhqk', qc, k_all, preferred_element_type=jnp.float32) * scale
        logits = logits + cq.transpose(0, 2, 1)[..., None] - c_k[:, :, None, :]
        mask = key_pos[None, :] <= pos[:, None]
        logits = jnp.where(mask[None, None], logits, -jnp.inf)
        p = jax.nn.softmax(logits, axis=-1)
        return jnp.einsum('bhqk,bkhd->bqhd', p, v_all.astype(jnp.float32)).astype(qc.dtype)

    nq = t // qb
    q_blocks = q.reshape(b, nq, qb, h, d).transpose(1, 0, 2, 3, 4)
    cq_blocks = c_q.reshape(b, nq, qb, h).transpose(1, 0, 2, 3)
    pos_blocks = (pos0 + jnp.arange(t)).reshape(nq, qb)
    out = lax.map(one_block, (q_blocks, cq_blocks, pos_blocks))
    return out.transpose(1, 0, 2, 3, 4).reshape(b, t, h, d)


def swiglu(x, w_gate, w_up, w_down):
    return (jax.nn.silu(x @ w_gate) * (x @ w_up)) @ w_down


def moe_swiglu(x, router_w, w_gate, w_up, w_down):
    logits = jnp.einsum('btd,de->bte', x, router_w, preferred_element_type=jnp.float32)
    probs = jax.nn.softmax(logits, axis=-1)
    top_p, top_i = lax.top_k(probs, TOP_K)
    top_p = top_p / top_p.sum(-1, keepdims=True)
    combine = jnp.sum(jax.nn.one_hot(top_i, N_EXPERTS, dtype=jnp.float32) * top_p[..., None], axis=-2)
    y = jnp.zeros(x.shape, jnp.float32)
    for e in range(N_EXPERTS):
        y = y + combine[..., e:e + 1] * swiglu(x, w_gate[e], w_up[e], w_down[e])
    return y.astype(x.dtype)


def hybrid_layer(x, pos0, past_mk, past_mv, past_fk, past_fv, past_flogf, conv_state,
                 w_in, conv_w, fox_fbias, w_out, ln_mix_g, ln_mix_b, ln_ffn_g, ln_ffn_b, channel_mixer):
    b, t, _ = x.shape
    widths = (CONV_W,) * 3 + (MOBA_W,) * 3 + (FOX_W,) * 3
    splits = [sum(widths[:i + 1]) for i in range(len(widths))]
    z = x @ w_in
    cb, cc, cx, mq, mk, mv, fq, fk, fv, fg = jnp.split(z, splits, axis=-1)

    u = cc * cx
    u_ext = jnp.concatenate([conv_state.astype(u.dtype), u], axis=1)
    y_conv = u_ext[:, 0:t] * conv_w[0]
    for i in range(1, CONV_K):
        y_conv = y_conv + u_ext[:, i:i + t] * conv_w[i]
    conv_out = cb * y_conv
    new_conv = u_ext[:, t:]

    qm = rope(mq.reshape(b, t, MOBA_H, HEAD_DIM), pos0)
    km = rope(mk.reshape(b, t, MOBA_H, HEAD_DIM), pos0)
    vm = mv.reshape(b, t, MOBA_H, HEAD_DIM)
    n_keys = pos0 + t
    nb = -(-n_keys // MOBA_BLOCK)
    pad = jnp.zeros((b, nb * MOBA_BLOCK - n_keys, MOBA_H, HEAD_DIM), x.dtype)
    km_all = jnp.concatenate([past_mk.astype(x.dtype), km, pad], axis=1)
    vm_all = jnp.concatenate([past_mv.astype(x.dtype), vm, pad], axis=1)
    out_m = moba_attention(qm, km_all, vm_all, pos0)

    qf = fq.reshape(b, t, FOX_H, HEAD_DIM)
    kf = fk.reshape(b, t, FOX_H, HEAD_DIM)
    vf = fv.reshape(b, t, FOX_H, HEAD_DIM)
    logf = jax.nn.log_sigmoid(fg.astype(jnp.float32) + fox_fbias.astype(jnp.float32)).astype(x.dtype)
    kf_all = jnp.concatenate([past_fk.astype(x.dtype), kf], axis=1)
    vf_all = jnp.concatenate([past_fv.astype(x.dtype), vf], axis=1)
    logf_all = jnp.concatenate([past_flogf.astype(x.dtype), logf], axis=1)
    out_f = forgetting_attention(qf, kf_all, vf_all, logf_all, pos0)

    mixed = jnp.concatenate([conv_out, out_m.reshape(b, t, MOBA_W), out_f.reshape(b, t, FOX_W)], axis=-1) @ w_out
    x = layer_norm(ALPHA * x + mixed, ln_mix_g, ln_mix_b)
    x = layer_norm(ALPHA * x + channel_mixer(x), ln_ffn_g, ln_ffn_b)
    return x, (km, vm, kf, vf, logf, new_conv)


def gather_pages(cache, l, page_table):
    rows = cache[l, page_table]
    return rows.reshape((page_table.shape[0], page_table.shape[1] * cache.shape[2]) + cache.shape[3:])


def setup_inputs(seed: int = 0) -> dict:
    key = jax.random.key(seed)
    keys = iter(jax.random.split(key, 32))
    f32 = jnp.float32
    n_pages = PAST_LEN // PAGE_SIZE
    n_used = DEC_BATCH * n_pages
    n_pool = n_used + n_used // 4

    def nrm(shape, scale):
        return jax.random.normal(next(keys), shape, f32) * scale

    x_prompt = nrm((BATCH, SEQ, D_MODEL), 1.0)
    x_sample = nrm((DEC_BATCH, DEC_SEQ, D_MODEL), 1.0)
    cache_moba_k = nrm((DEPTH, n_pool, PAGE_SIZE, MOBA_H, HEAD_DIM), 1.0)
    cache_moba_v = nrm((DEPTH, n_pool, PAGE_SIZE, MOBA_H, HEAD_DIM), 1.0)
    cache_fox_k = nrm((DEPTH, n_pool, PAGE_SIZE, FOX_H, HEAD_DIM), 1.0)
    cache_fox_v = nrm((DEPTH, n_pool, PAGE_SIZE, FOX_H, HEAD_DIM), 1.0)
    cache_fox_logf = jax.nn.log_sigmoid(FORGET_BIAS_INIT + nrm((DEPTH, n_pool, PAGE_SIZE, FOX_H), 1.0))
    state_conv = nrm((DEPTH, DEC_BATCH, CONV_K - 1, CONV_W), 1.0)
    page_table = jax.random.permutation(next(keys), n_pool)[:n_used].reshape(DEC_BATCH, n_pages).astype(jnp.int32)

    w_in = nrm((DEPTH, D_MODEL, IN_W), D_MODEL ** -0.5)
    conv_w = nrm((DEPTH, CONV_K, CONV_W), CONV_K ** -0.5)
    fox_fbias = FORGET_BIAS_INIT + nrm((DEPTH, FOX_H), 0.1)
    w_out = nrm((DEPTH, MIX_W, D_MODEL), MIX_W ** -0.5 * BETA)
    ln_mix_g = 1.0 + nrm((DEPTH, D_MODEL), 0.02)
    ln_mix_b = nrm((DEPTH, D_MODEL), 0.02)
    ln_ffn_g = 1.0 + nrm((DEPTH, D_MODEL), 0.02)
    ln_ffn_b = nrm((DEPTH, D_MODEL), 0.02)
    dense_w_gate = nrm((N_DENSE, D_MODEL, D_FF), D_MODEL ** -0.5)
    dense_w_up = nrm((N_DENSE, D_MODEL, D_FF), D_MODEL ** -0.5)
    dense_w_down = nrm((N_DENSE, D_FF, D_MODEL), D_FF ** -0.5 * BETA)
    router_w = nrm((N_MOE, D_MODEL, N_EXPERTS), D_MODEL ** -0.5)
    moe_w_gate = nrm((N_MOE, N_EXPERTS, D_MODEL, EXPERT_D_FF), D_MODEL ** -0.5)
    moe_w_up = nrm((N_MOE, N_EXPERTS, D_MODEL, EXPERT_D_FF), D_MODEL ** -0.5)
    moe_w_down = nrm((N_MOE, N_EXPERTS, EXPERT_D_FF, D_MODEL), EXPERT_D_FF ** -0.5 * BETA)
    return {'x_prompt': x_prompt, 'x_sample': x_sample,
            'cache_moba_k': cache_moba_k, 'cache_moba_v': cache_moba_v,
            'cache_fox_k': cache_fox_k, 'cache_fox_v': cache_fox_v, 'cache_fox_logf': cache_fox_logf,
            'state_conv': state_conv, 'page_table': page_table,
            'w_in': w_in, 'conv_w': conv_w, 'fox_fbias': fox_fbias, 'w_out': w_out,
            'ln_mix_g': ln_mix_g, 'ln_mix_b': ln_mix_b, 'ln_ffn_g': ln_ffn_g, 'ln_ffn_b': ln_ffn_b,
            'dense_w_gate': dense_w_gate, 'dense_w_up': dense_w_up, 'dense_w_down': dense_w_down,
            'router_w': router_w, 'moe_w_gate': moe_w_gate, 'moe_w_up': moe_w_up, 'moe_w_down': moe_w_down}


def reference(x_prompt, x_sample, cache_moba_k, cache_moba_v, cache_fox_k, cache_fox_v, cache_fox_logf,
              state_conv, page_table, w_in, conv_w, fox_fbias, w_out, ln_mix_g, ln_mix_b, ln_ffn_g, ln_ffn_b,
              dense_w_gate, dense_w_up, dense_w_down, router_w, moe_w_gate, moe_w_up, moe_w_down):
    b_p = x_prompt.shape[0]
    hp, hs = x_prompt, x_sample
    rows_p, rows_s = [], []
    for l in range(DEPTH):
        i = l // 2
        if l % 2 == 0:
            ffn = functools.partial(swiglu, w_gate=dense_w_gate[i], w_up=dense_w_up[i], w_down=dense_w_down[i])
        else:
            ffn = functools.partial(moe_swiglu, router_w=router_w[i], w_gate=moe_w_gate[i],
                                    w_up=moe_w_up[i], w_down=moe_w_down[i])
        lw = (w_in[l], conv_w[l], fox_fbias[l], w_out[l], ln_mix_g[l], ln_mix_b[l], ln_ffn_g[l], ln_ffn_b[l], ffn)
        empty_m = jnp.zeros((b_p, 0, MOBA_H, HEAD_DIM), hp.dtype)
        empty_f = jnp.zeros((b_p, 0, FOX_H, HEAD_DIM), hp.dtype)
        empty_g = jnp.zeros((b_p, 0, FOX_H), hp.dtype)
        conv0 = jnp.zeros((b_p, CONV_K - 1, CONV_W), hp.dtype)
        hp, r_p = hybrid_layer(hp, 0, empty_m, empty_m, empty_f, empty_f, empty_g, conv0, *lw)
        rows_p.append(r_p)
        hs, r_s = hybrid_layer(hs, PAST_LEN,
                               gather_pages(cache_moba_k, l, page_table), gather_pages(cache_moba_v, l, page_table),
                               gather_pages(cache_fox_k, l, page_table), gather_pages(cache_fox_v, l, page_table),
                               gather_pages(cache_fox_logf, l, page_table), state_conv[l], *lw)
        rows_s.append(r_s)
    return (hp, hs,
            jnp.stack([r[0] for r in rows_p]), jnp.stack([r[1] for r in rows_p]),
            jnp.stack([r[2] for r in rows_p]), jnp.stack([r[3] for r in rows_p]),
            jnp.stack([r[4] for r in rows_p]), jnp.stack([r[5] for r in rows_p]),
            jnp.stack([r[0] for r in rows_s]), jnp.stack([r[1] for r in rows_s]),
            jnp.stack([r[2] for r in rows_s]), jnp.stack([r[3] for r in rows_s]),
            jnp.stack([r[4] for r in rows_s]), jnp.stack([r[5] for r in rows_s]))
```

```python
import functools

import jax
import jax.numpy as jnp
from jax import lax
from jax.experimental import pallas as pl
from jax.experimental.pallas import tpu as pltpu

F32 = jnp.float32
BF16 = jnp.bfloat16

HEAD_DIM = 64
HALF = HEAD_DIM // 2
CONV_W = 256
CONV_K = 3
MOBA_H = 6
FOX_H = 6
MOBA_W = MOBA_H * HEAD_DIM
FOX_W = FOX_H * HEAD_DIM
MOBA_BLOCK = 256
MOBA_TOPK = 3
ROPE_THETA = 10000.0
LN_EPS = 1e-5
TOP_K = 2
LANES = 128
SUBLANES = 8
NEG_BIG = -1e30
VMEM_LIMIT = 56 * 1024 * 1024
ROUTE_CHUNK = 128
ROUTE_GROUP = 4
ROUTE_TILE = 1024

C_CB, C_CC, C_CX = 0, CONV_W, 2 * CONV_W
C_MQ = 3 * CONV_W
C_MK = C_MQ + MOBA_W
C_MV = C_MK + MOBA_W
C_FQ = C_MV + MOBA_W
C_FK = C_FQ + FOX_W
C_FV = C_FK + FOX_W
C_FG = C_FV + FOX_W


def _cparams(sem):
    return pltpu.CompilerParams(dimension_semantics=sem, vmem_limit_bytes=VMEM_LIMIT)


def _split2(a):
    hi = a.astype(BF16)
    lo = (a - hi.astype(F32)).astype(BF16)
    return hi, lo


def _split3(a):
    p0 = a.astype(BF16)
    r = a - p0.astype(F32)
    p1 = r.astype(BF16)
    p2 = (r - p1.astype(F32)).astype(BF16)
    return p0, p1, p2


_NN = (((1,), (0,)), ((), ()))
_NT = (((1,), (1,)), ((), ()))


def _mm(a, b, dims=_NN):
    return lax.dot_general(a, b, dims, preferred_element_type=F32)


def _dot(a, b, hi, dims=_NN):
    if not hi:
        return _mm(a.astype(BF16), b.astype(BF16), dims)
    a0, a1 = _split2(a.astype(F32))
    b0, b1 = _split2(b.astype(F32))
    return _mm(a0, b0, dims) + (_mm(a0, b1, dims) + _mm(a1, b0, dims))


def _dot_exact_rhs(a, b_exact, dims=_NN):
    p0, p1, p2 = _split3(a)
    return _mm(p0, b_exact, dims) + (_mm(p1, b_exact, dims) + _mm(p2, b_exact, dims))


def _log_sigmoid(x):
    return jnp.minimum(x, 0.0) - jnp.log1p(jnp.exp(-jnp.abs(x)))


def _silu(x):
    return x * (1.0 / (1.0 + jnp.exp(-x)))


def _layer_norm(y, g, b):
    mu = jnp.mean(y, axis=-1, keepdims=True)
    d = y - mu
    var = jnp.mean(d * d, axis=-1, keepdims=True)
    return d * lax.rsqrt(var + LN_EPS) * g + b


def _rope_rows(x, cos, sin_signed):
    n = x.shape[-1]
    lane = lax.broadcasted_iota(jnp.int32, x.shape, 1)
    first = (lane & (HEAD_DIM - 1)) < HALF
    rot = jnp.where(first, pltpu.roll(x, n - HALF, 1), pltpu.roll(x, HALF, 1))
    return x * cos + rot * sin_signed


def _top_select(g, n_sel):
    lane = lax.broadcasted_iota(jnp.int32, g.shape, 1).astype(F32)
    sel = jnp.zeros(g.shape, F32)
    for _ in range(n_sel):
        m = jnp.max(g, axis=1, keepdims=True)
        idx = jnp.min(jnp.where(g == m, lane, float(g.shape[1])), axis=1, keepdims=True)
        pick = lane == idx
        sel = jnp.where(pick, jnp.where(m > -jnp.inf, 1.0, sel), sel)
        g = jnp.where(pick, -jnp.inf, g)
    return sel


def _inproj_prompt_kernel(x_ref, wr_ref, wt_ref, fb_ref, cq_ref, sq_ref, ct_ref, st_ref,
                          r_ref, t_ref, lf_ref):
    x = x_ref[...].astype(BF16)
    r = _mm(x, wr_ref[...])
    t = _mm(wt_ref[...], x, _NT)
    nq = 3 * CONV_W
    r_ref[:, 0:nq] = r[:, 0:nq]
    r_ref[:, nq:nq + MOBA_W] = _rope_rows(r[:, nq:nq + MOBA_W], cq_ref[...], sq_ref[...])
    r_ref[:, nq + MOBA_W:] = r[:, nq + MOBA_W:]
    cos = ct_ref[...]
    sin = st_ref[...]
    for h in range(MOBA_H):
        a = h * HEAD_DIM
        x1 = t[a:a + HALF]
        x2 = t[a + HALF:a + HEAD_DIM]
        t_ref[0, a:a + HALF, :] = x1 * cos - x2 * sin
        t_ref[0, a + HALF:a + HEAD_DIM, :] = x2 * cos + x1 * sin
    nt = 2 * MOBA_W + 2 * FOX_W
    t_ref[0, MOBA_W:nt, :] = t[MOBA_W:nt]
    lf_ref[0] = _log_sigmoid(t[nt:nt + SUBLANES] + fb_ref[...])


def _inproj_prompt(x, wr, wt, fb, cosq, sinq, cost, sint, batch, seq, tm):
    n, d = x.shape
    nts = seq // tm
    nr = wr.shape[1]
    ntr = wt.shape[0]
    nt = ntr - SUBLANES
    return pl.pallas_call(
        _inproj_prompt_kernel,
        grid=(n // tm,),
        in_specs=[
            pl.BlockSpec((tm, d), lambda i: (i, 0)),
            pl.BlockSpec((d, nr), lambda i: (0, 0)),
            pl.BlockSpec((ntr, d), lambda i: (0, 0)),
            pl.BlockSpec((SUBLANES, 1), lambda i: (0, 0)),
            pl.BlockSpec((tm, MOBA_W), lambda i: (i % nts, 0)),
            pl.BlockSpec((tm, MOBA_W), lambda i: (i % nts, 0)),
            pl.BlockSpec((HALF, tm), lambda i: (0, i % nts)),
            pl.BlockSpec((HALF, tm), lambda i: (0, i % nts)),
        ],
        out_specs=[
            pl.BlockSpec((tm, nr), lambda i: (i, 0)),
            pl.BlockSpec((1, nt, tm), lambda i: (i // nts, 0, i % nts)),
            pl.BlockSpec((1, SUBLANES, tm), lambda i: (i // nts, 0, i % nts)),
        ],
        out_shape=[
            jax.ShapeDtypeStruct((n, nr), F32),
            jax.ShapeDtypeStruct((batch, nt, seq), F32),
            jax.ShapeDtypeStruct((batch, SUBLANES, seq), F32),
        ],
        compiler_params=_cparams(("arbitrary",)),
        name="inproj_prompt",
    )(x, wr, wt, fb, cosq, sinq, cost, sint)


def _inproj_sample_kernel(x_ref, w_ref, fb_ref, cq_ref, sq_ref, z_ref):
    j = pl.program_id(0)
    z = _dot(x_ref[...], w_ref[...], True)
    roped = _rope_rows(z, cq_ref[...], sq_ref[...])
    gate = _log_sigmoid(z + fb_ref[...])
    is_rope = jnp.logical_or(j == C_MQ // MOBA_W, j == C_MK // MOBA_W)
    z_ref[...] = jnp.where(is_rope, roped, jnp.where(j == C_FG // MOBA_W, gate, z))


def _inproj_sample(x, w, fb, cosq, sinq):
    n, d = x.shape
    tn = MOBA_W
    return pl.pallas_call(
        _inproj_sample_kernel,
        grid=(w.shape[1] // tn,),
        in_specs=[
            pl.BlockSpec((n, d), lambda j: (0, 0)),
            pl.BlockSpec((d, tn), lambda j: (0, j)),
            pl.BlockSpec((1, tn), lambda j: (0, 0)),
            pl.BlockSpec((n, tn), lambda j: (0, 0)),
            pl.BlockSpec((n, tn), lambda j: (0, 0)),
        ],
        out_specs=pl.BlockSpec((n, tn), lambda j: (0, j)),
        out_shape=jax.ShapeDtypeStruct((n, w.shape[1]), F32),
        compiler_params=_cparams(("arbitrary",)),
        name="inproj_sample",
    )(x, w, fb, cosq, sinq)


def _attn_prompt_kernel(*refs, fox, nblk):
    if fox:
        q_ref, kt_ref, vt_ref, lf_ref, tri_ref, o_ref, c_ref = refs
    else:
        q_ref, kt_ref, vt_ref, o_ref, km_ref = refs
    p = pl.program_id(1)
    i = pl.program_id(2)
    tq = q_ref.shape[0]
    bs = MOBA_BLOCK
    lane = lax.broadcasted_iota(jnp.int32, (tq, LANES), 1)
    q = q_ref[...] * (HEAD_DIM ** -0.5)
    qh = (jnp.where(lane < HEAD_DIM, q, 0.0), jnp.where(lane < HEAD_DIM, 0.0, q))
    qb = tuple(v.astype(BF16) for v in qh)

    if fox:
        @pl.when(i == 0)
        def _():
            carry = jnp.zeros((SUBLANES, 1), F32)
            for j in range(nblk):
                blk = _dot_exact_rhs(lf_ref[0, :, j * bs:(j + 1) * bs], tri_ref[...]) + carry
                c_ref[:, j * bs:(j + 1) * bs] = blk
                carry = blk[:, bs - 1:bs]
        sel = None
    else:
        @pl.when(i == 0)
        def _():
            km = jnp.zeros((LANES, LANES), F32)
            kl = lax.broadcasted_iota(jnp.int32, (LANES, LANES), 1)
            for j in range(nblk):
                col = jnp.mean(kt_ref[:, j * bs:(j + 1) * bs], axis=1, keepdims=True)
                km = jnp.where(kl == j, col, km)
            km_ref[...] = km
        sel = []
        for h in range(2):
            g = _dot(qh[h], km_ref[...], True)
            g = jnp.where(lane < i, g, -jnp.inf)
            sel.append(_top_select(g, min(MOBA_TOPK, nblk)))

    def bias_rows(h, off):
        row = c_ref[pl.ds(2 * p + h, 1), pl.ds(off, bs)]
        return -row

    def step(off, j, state, diag):
        kt = kt_ref[:, pl.ds(off, bs)].astype(BF16)
        vt = vt_ref[:, pl.ds(off, bs)].astype(BF16)
        out = []
        for h in range(2):
            m, l, acc = state[h]
            s = _mm(qb[h], kt)
            if fox:
                s = s + bias_rows(h, off)
            else:
                if not diag:
                    on = jnp.max(jnp.where(lane == j, sel[h], 0.0), axis=1, keepdims=True)
                    s = jnp.where(on > 0.0, s, -jnp.inf)
            if diag:
                r_i = lax.broadcasted_iota(jnp.int32, (tq, bs), 0)
                c_i = lax.broadcasted_iota(jnp.int32, (tq, bs), 1)
                s = jnp.where(c_i <= r_i, s, -jnp.inf)
            m_new = jnp.maximum(m, jnp.max(s, axis=1, keepdims=True))
            alpha = jnp.exp(m - m_new)
            pexp = jnp.exp(s - m_new)
            l = l * alpha + jnp.sum(pexp, axis=1, keepdims=True)
            acc = acc * alpha + _mm(pexp.astype(BF16), vt, _NT)
            out.append((m_new, l, acc))
        return tuple(out)

    init = tuple((jnp.full((tq, 1), NEG_BIG, F32), jnp.zeros((tq, 1), F32), jnp.zeros((tq, LANES), F32))
                 for _ in range(2))
    state = step(pl.multiple_of(i * bs, bs), i, init, True)

    def body(j, st):
        return step(pl.multiple_of(j * bs, bs), j, st, False)

    state = lax.fori_loop(0, i, body, state)
    o0 = state[0][2] / state[0][1]
    o1 = state[1][2] / state[1][1]
    o_ref[...] = jnp.where(lane < HEAD_DIM, o0, o1)


def _attn_prompt(r, tm_arr, lf, tri, batch, seq, fox):
    n = r.shape[0]
    tq = MOBA_BLOCK
    nq = seq // tq
    npair = (FOX_H if fox else MOBA_H) // 2
    qcol0 = (3 * CONV_W + (MOBA_W if fox else 0)) // LANES
    krow0 = (2 * MOBA_W if fox else 0) // LANES
    vrow0 = krow0 + (FOX_W if fox else MOBA_W) // LANES
    in_specs = [
        pl.BlockSpec((tq, LANES), lambda b, p, i: (b * nq + i, qcol0 + p)),
        pl.BlockSpec((None, LANES, seq), lambda b, p, i: (b, krow0 + p, 0)),
        pl.BlockSpec((None, LANES, seq), lambda b, p, i: (b, vrow0 + p, 0)),
    ]
    args = [r, tm_arr, tm_arr]
    if fox:
        in_specs += [pl.BlockSpec((1, SUBLANES, seq), lambda b, p, i: (b, 0, 0)),
                     pl.BlockSpec((tq, tq), lambda b, p, i: (0, 0))]
        args += [lf, tri]
        scratch = [pltpu.VMEM((SUBLANES, seq), F32)]
    else:
        scratch = [pltpu.VMEM((LANES, LANES), F32)]
    return pl.pallas_call(
        functools.partial(_attn_prompt_kernel, fox=fox, nblk=nq),
        grid=(batch, npair, nq),
        in_specs=in_specs,
        out_specs=pl.BlockSpec((tq, LANES), lambda b, p, i: (b * nq + i, p)),
        out_shape=jax.ShapeDtypeStruct((n, npair * LANES), F32),
        scratch_shapes=scratch,
        compiler_params=_cparams(("arbitrary", "arbitrary", "arbitrary")),
        name="fox_prompt" if fox else "moba_prompt",
    )(*args)


def _block_diag_q(q):
    nh = q.shape[1] // HEAD_DIM
    rows = nh * q.shape[0]
    qt = jnp.concatenate([q] * nh, axis=0)
    r_i = lax.broadcasted_iota(jnp.int32, (rows, q.shape[1]), 0)
    c_i = lax.broadcasted_iota(jnp.int32, (rows, q.shape[1]), 1)
    return jnp.where(r_i // q.shape[0] == c_i // HEAD_DIM, qt, 0.0)


def _diag_heads(o, t):
    nh = o.shape[1] // HEAD_DIM
    c_i = lax.broadcasted_iota(jnp.int32, (t, o.shape[1]), 1)
    out = jnp.zeros((t, o.shape[1]), F32)
    for h in range(nh):
        out = jnp.where(c_i // HEAD_DIM == h, o[h * t:(h + 1) * t], out)
    return out


def _expand_heads(c, t):
    nh = FOX_H
    return jnp.concatenate([jnp.broadcast_to(c[h:h + 1], (t, c.shape[1])) for h in range(nh)], axis=0)


def _softmax_update(state, s, v, dims, hi=False):
    m, l, acc = state
    m_new = jnp.maximum(m, jnp.max(s, axis=1, keepdims=True))
    alpha = jnp.exp(m - m_new)
    pexp = jnp.exp(s - m_new)
    l = l * alpha + jnp.sum(pexp, axis=1, keepdims=True)
    acc = acc * alpha + _dot(pexp, v, hi, dims)
    return m_new, l, acc


def _fox_decode_kernel(pt_ref, q_ref, kn_ref, vn_ref, lfn_ref, tri_ref, tri8_ref, *rest, npg, tsz):
    kt_refs = rest[0:npg]
    vt_refs = rest[npg:2 * npg]
    lf_refs = rest[2 * npg:3 * npg]
    o_ref, m_ref, l_ref, acc_ref, carry_ref = rest[3 * npg:]
    b = pl.program_id(0)
    c = pl.program_id(1)
    nc = pl.num_programs(1)
    rows = FOX_H * tsz

    @pl.when(c == 0)
    def _():
        m_ref[...] = jnp.full(m_ref.shape, NEG_BIG, F32)
        l_ref[...] = jnp.zeros(l_ref.shape, F32)
        acc_ref[...] = jnp.zeros(acc_ref.shape, F32)
        carry_ref[...] = jnp.zeros(carry_ref.shape, F32)

    qbd = _block_diag_q(q_ref[...] * (HEAD_DIM ** -0.5))
    state = (m_ref[:, 0:1], l_ref[:, 0:1], acc_ref[...])
    carry = carry_ref[:, 0:1]
    sub = lax.broadcasted_iota(jnp.int32, (SUBLANES, LANES), 0)
    for k in range(npg):
        page = pt_ref[b, c * npg + k]
        r = page % SUBLANES
        lf = jnp.zeros((SUBLANES, LANES), F32)
        for h in range(FOX_H):
            row = lf_refs[k][h, pl.ds(r, 1), :]
            lf = jnp.where(sub == h, jnp.broadcast_to(row, (SUBLANES, LANES)), lf)
        ck = _dot_exact_rhs(lf, tri_ref[...]) + carry
        carry = ck[:, LANES - 1:LANES]
        hi = k >= npg - 2
        s = _dot(qbd, kt_refs[k][...], hi) - _expand_heads(ck, tsz)
        state = _softmax_update(state, s, vt_refs[k][...], _NT, hi)

    @pl.when(c < nc - 1)
    def _():
        m_ref[...] = jnp.broadcast_to(state[0], m_ref.shape)
        l_ref[...] = jnp.broadcast_to(state[1], l_ref.shape)
        acc_ref[...] = state[2]
        carry_ref[...] = jnp.broadcast_to(carry, carry_ref.shape)

    @pl.when(c == nc - 1)
    def _():
        cn = _dot_exact_rhs(lfn_ref[...], tri8_ref[...]) + carry
        s = _dot(qbd, kn_ref[...], True, _NT) - _expand_heads(cn, tsz)
        r_i = lax.broadcasted_iota(jnp.int32, (rows, tsz), 0)
        c_i = lax.broadcasted_iota(jnp.int32, (rows, tsz), 1)
        s = jnp.where(c_i <= r_i % tsz, s, -jnp.inf)
        _, l, acc = _softmax_update(state, s, vn_ref[...], _NN, True)
        o_ref[...] = _diag_heads(acc / l, tsz)


def _fox_decode(pt, zs, lfn, cache_k, cache_v, cache_lf, tri, tri8, layer, npg):
    nseq, npages = pt.shape
    tsz = zs.shape[0] // nseq
    nc = npages // npg
    rows = FOX_H * tsz

    def page_spec(k):
        return pl.BlockSpec((None, None, FOX_W, LANES), lambda b, c, pt: (layer, pt[b, c * npg + k], 0, 0))

    def lf_spec(k):
        return pl.BlockSpec((None, FOX_H, SUBLANES, LANES),
                            lambda b, c, pt: (layer, 0, pt[b, c * npg + k] // SUBLANES, 0))

    in_specs = [
        pl.BlockSpec((tsz, FOX_W), lambda b, c, pt: (b, C_FQ // FOX_W)),
        pl.BlockSpec((tsz, FOX_W), lambda b, c, pt: (b, C_FK // FOX_W)),
        pl.BlockSpec((tsz, FOX_W), lambda b, c, pt: (b, C_FV // FOX_W)),
        pl.BlockSpec((None, SUBLANES, tsz), lambda b, c, pt: (b, 0, 0)),
        pl.BlockSpec((LANES, LANES), lambda b, c, pt: (0, 0)),
        pl.BlockSpec((tsz, tsz), lambda b, c, pt: (0, 0)),
    ]
    in_specs += [page_spec(k) for k in range(npg)]
    in_specs += [page_spec(k) for k in range(npg)]
    in_specs += [lf_spec(k) for k in range(npg)]
    grid_spec = pltpu.PrefetchScalarGridSpec(
        num_scalar_prefetch=1,
        grid=(nseq, nc),
        in_specs=in_specs,
        out_specs=pl.BlockSpec((tsz, FOX_W), lambda b, c, pt: (b, 0)),
        scratch_shapes=[pltpu.VMEM((rows, LANES), F32), pltpu.VMEM((rows, LANES), F32),
                        pltpu.VMEM((rows, FOX_W), F32), pltpu.VMEM((SUBLANES, LANES), F32)],
    )
    return pl.pallas_call(
        functools.partial(_fox_decode_kernel, npg=npg, tsz=tsz),
        grid_spec=grid_spec,
        out_shape=jax.ShapeDtypeStruct((nseq * tsz, FOX_W), F32),
        compiler_params=_cparams(("arbitrary", "arbitrary")),
        name="fox_decode",
    )(pt, zs, zs, zs, lfn, tri, tri8, *([cache_k] * npg), *([cache_v] * npg), *([cache_lf] * npg))


def _moba_decode_kernel(pt_ref, q_ref, kn_ref, vn_ref, *rest, npg, tsz):
    kt_refs = rest[0:npg]
    vt_refs = rest[npg:2 * npg]
    o_ref, s_ref, g_ref, sel_ref, m_ref, l_ref, acc_ref = rest[2 * npg:]
    ph = pl.program_id(1)
    c = pl.program_id(2)
    nc = pl.num_programs(2)
    rows = MOBA_H * tsz
    ppb = MOBA_BLOCK // LANES
    qbd = _block_diag_q(q_ref[...] * (HEAD_DIM ** -0.5)).astype(BF16)
    lane = lax.broadcasted_iota(jnp.int32, (rows, LANES), 1)

    @pl.when(ph == 0)
    def _():
        @pl.when(c == 0)
        def _():
            g_ref[...] = jnp.full(g_ref.shape, -jnp.inf, F32)
        g = g_ref[...]
        for jb in range(npg // ppb):
            tot = jnp.zeros((rows, 1), F32)
            for k in range(jb * ppb, (jb + 1) * ppb):
                s = _mm(qbd, kt_refs[k][...].astype(BF16))
                s_ref[:, pl.ds(pl.multiple_of((c * npg + k) * LANES, LANES), LANES)] = s
                tot = tot + jnp.sum(s, axis=1, keepdims=True)
            g = jnp.where(lane == c * (npg // ppb) + jb, tot * (1.0 / MOBA_BLOCK), g)
        g_ref[...] = g

    @pl.when(ph == 1)
    def _():
        @pl.when(c == 0)
        def _():
            sel_ref[...] = _top_select(g_ref[...], MOBA_TOPK)
            s = _mm(qbd, kn_ref[...].astype(BF16), _NT)
            r_i = lax.broadcasted_iota(jnp.int32, (rows, tsz), 0)
            c_i = lax.broadcasted_iota(jnp.int32, (rows, tsz), 1)
            s = jnp.where(c_i <= r_i % tsz, s, -jnp.inf)
            init = (jnp.full((rows, 1), NEG_BIG, F32), jnp.zeros((rows, 1), F32),
                    jnp.zeros((rows, MOBA_W), F32))
            m, l, acc = _softmax_update(init, s, vn_ref[...], _NN)
            m_ref[...] = jnp.broadcast_to(m, m_ref.shape)
            l_ref[...] = jnp.broadcast_to(l, l_ref.shape)
            acc_ref[...] = acc

        state = (m_ref[:, 0:1], l_ref[:, 0:1], acc_ref[...])
        sel = sel_ref[...]
        for jb in range(npg // ppb):
            on = jnp.max(jnp.where(lane == c * (npg // ppb) + jb, sel, 0.0), axis=1, keepdims=True)
            for k in range(jb * ppb, (jb + 1) * ppb):
                s = s_ref[:, pl.ds(pl.multiple_of((c * npg + k) * LANES, LANES), LANES)]
                s = jnp.where(on > 0.0, s, -jnp.inf)
                state = _softmax_update(state, s, vt_refs[k][...], _NT)
        m_ref[...] = jnp.broadcast_to(state[0], m_ref.shape)
        l_ref[...] = jnp.broadcast_to(state[1], l_ref.shape)
        acc_ref[...] = state[2]

        @pl.when(c == nc - 1)
        def _():
            o_ref[...] = _diag_heads(state[2] / state[1], tsz)


def _moba_decode(pt, zs, cache_k, cache_v, layer, npg):
    nseq, npages = pt.shape
    tsz = zs.shape[0] // nseq
    nc = npages // npg
    rows = MOBA_H * tsz

    def k_spec(k):
        return pl.BlockSpec((None, None, MOBA_W, LANES),
                            lambda b, ph, c, pt: (layer, pt[b, (c * (1 - ph) + (nc - 1) * ph) * npg + k], 0, 0))

    def v_spec(k):
        return pl.BlockSpec((None, None, MOBA_W, LANES), lambda b, ph, c, pt: (layer, pt[b, c * ph * npg + k], 0, 0))

    in_specs = [
        pl.BlockSpec((tsz, MOBA_W), lambda b, ph, c, pt: (b, C_MQ // MOBA_W)),
        pl.BlockSpec((tsz, MOBA_W), lambda b, ph, c, pt: (b, C_MK // MOBA_W)),
        pl.BlockSpec((tsz, MOBA_W), lambda b, ph, c, pt: (b, C_MV // MOBA_W)),
    ]
    in_specs += [k_spec(k) for k in range(npg)]
    in_specs += [v_spec(k) for k in range(npg)]
    grid_spec = pltpu.PrefetchScalarGridSpec(
        num_scalar_prefetch=1,
        grid=(nseq, 2, nc),
        in_specs=in_specs,
        out_specs=pl.BlockSpec((tsz, MOBA_W), lambda b, ph, c, pt: (b, 0)),
        scratch_shapes=[pltpu.VMEM((rows, npages * LANES), F32), pltpu.VMEM((rows, LANES), F32),
                        pltpu.VMEM((rows, LANES), F32), pltpu.VMEM((rows, LANES), F32),
                        pltpu.VMEM((rows, LANES), F32), pltpu.VMEM((rows, MOBA_W), F32)],
    )
    return pl.pallas_call(
        functools.partial(_moba_decode_kernel, npg=npg, tsz=tsz),
        grid_spec=grid_spec,
        out_shape=jax.ShapeDtypeStruct((nseq * tsz, MOBA_W), F32),
        compiler_params=_cparams(("arbitrary", "arbitrary", "arbitrary")),
        name="moba_decode",
    )(pt, zs, zs, zs, *([cache_k] * npg), *([cache_v] * npg))


def _outproj_kernel(*refs, alpha, sample, hi, nts):
    if sample:
        x_ref, cb_ref, cc_ref, cx_ref, h0_ref, h1_ref, mo_ref, fo_ref, wo_ref, cw_ref, g_ref, b_ref, \
            y_ref, u_ref = refs
    else:
        x_ref, cb_ref, cc_ref, cx_ref, cch_ref, cxh_ref, mo_ref, fo_ref, wo_ref, cw_ref, g_ref, b_ref, \
            y_ref, u_ref = refs
    u = cc_ref[...] * cx_ref[...]
    tm = u.shape[0]
    row = lax.broadcasted_iota(jnp.int32, u.shape, 0)
    r1 = pltpu.roll(u, 1, 0)
    r2 = pltpu.roll(u, 2, 0)
    if sample:
        t = row % SUBLANES
        s1 = jnp.where(t == 0, h1_ref[...], r1)
        s2 = jnp.where(t == 0, h0_ref[...], jnp.where(t == 1, h1_ref[...], r2))
        u_ref[...] = u
    else:
        first = pl.program_id(0) % nts == 0
        uh = jnp.where(first, 0.0, cch_ref[...] * cxh_ref[...])
        p1 = jnp.broadcast_to(uh[SUBLANES - 1:SUBLANES], u.shape)
        p2 = jnp.broadcast_to(uh[SUBLANES - 2:SUBLANES - 1], u.shape)
        s1 = jnp.where(row == 0, p1, r1)
        s2 = jnp.where(row == 0, p2, jnp.where(row == 1, p1, r2))
        u_ref[...] = u[tm - SUBLANES:tm]
    cw = cw_ref[...]
    conv = cb_ref[...] * (s2 * cw[0:1] + s1 * cw[1:2] + u * cw[2:3])
    mixed = (_dot(conv, wo_ref[0:CONV_W, :], hi)
             + _dot(mo_ref[...], wo_ref[CONV_W:CONV_W + MOBA_W, :], hi)
             + _dot(fo_ref[...], wo_ref[CONV_W + MOBA_W:, :], hi))
    y_ref[...] = _layer_norm(alpha * x_ref[...] + mixed, g_ref[...], b_ref[...])


def _outproj(x, r, ccol, mo, fo, wo, cw, g, b, alpha, tm, seq, hist=None):
    n, d = x.shape
    sample = hist is not None
    nts = 1 if sample else seq // tm
    hb = tm // SUBLANES
    col = lambda k: pl.BlockSpec((tm, CONV_W), lambda i: (i, ccol + k))
    if sample:
        extra = [pl.BlockSpec((tm, CONV_W), lambda i: (i, 0)), pl.BlockSpec((tm, CONV_W), lambda i: (i, 0))]
        extra_args = list(hist)
        u_shape, u_spec = (n, CONV_W), pl.BlockSpec((tm, CONV_W), lambda i: (i, 0))
    else:
        halo = lambda k: pl.BlockSpec((SUBLANES, CONV_W), lambda i: (jnp.maximum(i * hb - 1, 0), ccol + k))
        extra = [halo(1), halo(2)]
        extra_args = [r, r]
        u_shape, u_spec = (n // tm * SUBLANES, CONV_W), pl.BlockSpec((SUBLANES, CONV_W), lambda i: (i, 0))
    return pl.pallas_call(
        functools.partial(_outproj_kernel, alpha=alpha, sample=sample, hi=sample, nts=nts),
        grid=(n // tm,),
        in_specs=[pl.BlockSpec((tm, d), lambda i: (i, 0)), col(0), col(1), col(2)] + extra + [
            pl.BlockSpec((tm, MOBA_W), lambda i: (i, 0)),
            pl.BlockSpec((tm, FOX_W), lambda i: (i, 0)),
            pl.BlockSpec(wo.shape, lambda i: (0, 0)),
            pl.BlockSpec(cw.shape, lambda i: (0, 0)),
            pl.BlockSpec((1, d), lambda i: (0, 0)),
            pl.BlockSpec((1, d), lambda i: (0, 0)),
        ],
        out_specs=[pl.BlockSpec((tm, d), lambda i: (i, 0)), u_spec],
        out_shape=[jax.ShapeDtypeStruct((n, d), F32), jax.ShapeDtypeStruct(u_shape, F32)],
        compiler_params=_cparams(("arbitrary",)),
        name="outproj_sample" if sample else "outproj_prompt",
    )(x, r, r, r, *extra_args, mo, fo, wo, cw, g, b)


def _ffn_kernel(src_ref, eid_ref, ns_ref, *refs, g, hi):
    x_refs = refs[:g]
    wg_ref, wu_ref, wd_ref, o_ref, acc_ref = refs[g:]
    i = pl.program_id(0)
    f = pl.program_id(1)
    last = f == pl.num_programs(1) - 1
    live = i < ns_ref[0]

    @pl.when(live)
    def _():
        x = x_refs[0][...] if g == 1 else jnp.concatenate([r[...] for r in x_refs], axis=0)
        h = _silu(_dot(x, wg_ref[...], hi)) * _dot(x, wu_ref[...], hi)
        part = _dot(h, wd_ref[...], hi)

        @pl.when(f == 0)
        def _():
            acc_ref[...] = part

        @pl.when(f > 0)
        def _():
            acc_ref[...] = acc_ref[...] + part

        @pl.when(last)
        def _():
            o_ref[...] = acc_ref[...].astype(o_ref.dtype)

    @pl.when(jnp.logical_and(jnp.logical_not(live), last))
    def _():
        o_ref[...] = jnp.zeros(o_ref.shape, o_ref.dtype)


def _ffn(x, src, eid, nsteps, wg, wu, wd, rows, g, tf, hi, out_dtype):
    d = x.shape[1]
    ff = wg.shape[2]
    ns = eid.shape[0]

    def x_spec(j):
        return pl.BlockSpec((rows, d), lambda i, f, src, eid, n: (src[i * g + j], 0))

    grid_spec = pltpu.PrefetchScalarGridSpec(
        num_scalar_prefetch=3,
        grid=(ns, ff // tf),
        in_specs=[x_spec(j) for j in range(g)] + [
            pl.BlockSpec((None, d, tf), lambda i, f, src, eid, n: (eid[i], 0, f)),
            pl.BlockSpec((None, d, tf), lambda i, f, src, eid, n: (eid[i], 0, f)),
            pl.BlockSpec((None, tf, d), lambda i, f, src, eid, n: (eid[i], f, 0)),
        ],
        out_specs=pl.BlockSpec((g * rows, d), lambda i, f, src, eid, n: (i, 0)),
        scratch_shapes=[pltpu.VMEM((g * rows, d), F32)],
    )
    return pl.pallas_call(
        functools.partial(_ffn_kernel, g=g, hi=hi),
        grid_spec=grid_spec,
        out_shape=jax.ShapeDtypeStruct((ns * g * rows, d), out_dtype),
        compiler_params=_cparams(("arbitrary", "arbitrary")),
        name="ffn_hi" if hi else "ffn",
    )(src, eid, nsteps, *([x] * g), wg, wu, wd)


def _router_kernel(x_ref, w_ref, cwt_ref, cnt_ref, *, n_exp):
    logits = _dot(x_ref[...], w_ref[...], True)
    lane = lax.broadcasted_iota(jnp.int32, logits.shape, 1)
    logits = jnp.where(lane < n_exp, logits, -jnp.inf)
    m = jnp.max(logits, axis=1, keepdims=True)
    e = jnp.exp(logits - m)
    probs = e / jnp.sum(e, axis=1, keepdims=True)
    sel = _top_select(jnp.where(lane < n_exp, probs, -jnp.inf), TOP_K)
    top = probs * sel
    cw = top / jnp.sum(top, axis=1, keepdims=True)
    cwt_ref[...] = jnp.transpose(cw)[0:SUBLANES]
    cnt_ref[...] = jnp.broadcast_to(jnp.sum(sel, axis=0, keepdims=True), cnt_ref.shape)


def _router(x, w, n_exp, tm):
    n, d = x.shape
    return pl.pallas_call(
        functools.partial(_router_kernel, n_exp=n_exp),
        grid=(n // tm,),
        in_specs=[pl.BlockSpec((tm, d), lambda i: (i, 0)), pl.BlockSpec((d, LANES), lambda i: (0, 0))],
        out_specs=[pl.BlockSpec((SUBLANES, tm), lambda i: (0, i)),
                   pl.BlockSpec((SUBLANES, LANES), lambda i: (i, 0))],
        out_shape=[jax.ShapeDtypeStruct((SUBLANES, n), F32),
                   jax.ShapeDtypeStruct((n // tm * SUBLANES, LANES), F32)],
        compiler_params=_cparams(("arbitrary",)),
        name="router",
    )(x, w)


def _slot_rows(cwt_ref, tri_ref):
    on = cwt_ref[...] > 0.0
    cum = _mm(jnp.where(on, 1.0, 0.0).astype(BF16), tri_ref[...])
    return jnp.where(on, cum - 1.0, -1.0)


def _pack_kernel(cnt_ref, x_ref, cwt_ref, tri_ref, xc_ref, xb_ref, pos_ref):
    t = pl.program_id(0)
    e = pl.program_id(1)
    tm = x_ref.shape[0]

    @pl.when(e == 0)
    def _():
        xb_ref[...] = x_ref[...].astype(BF16)
        pos_ref[...] = _slot_rows(cwt_ref, tri_ref)

    row = pos_ref[pl.ds(e, 1), :]
    sub = lax.broadcasted_iota(jnp.int32, (ROUTE_CHUNK, tm), 0).astype(F32)
    for k in range(tm // ROUTE_CHUNK):
        lo = k * ROUTE_CHUNK

        @pl.when(cnt_ref[t, e] > lo)
        def _():
            sel = jnp.where(row - float(lo) == sub, 1.0, 0.0).astype(BF16)
            xc_ref[lo:lo + ROUTE_CHUNK, :] = _mm(sel, xb_ref[...]).astype(BF16)

        @pl.when(cnt_ref[t, e] <= lo)
        def _():
            xc_ref[lo:lo + ROUTE_CHUNK, :] = jnp.zeros((ROUTE_CHUNK, xc_ref.shape[1]), BF16)


def _pack(x, cwt, counts, tri, tm):
    n, d = x.shape
    n_exp = counts.shape[1]
    grid_spec = pltpu.PrefetchScalarGridSpec(
        num_scalar_prefetch=1,
        grid=(n // tm, n_exp),
        in_specs=[
            pl.BlockSpec((tm, d), lambda t, e, c: (t, 0)),
            pl.BlockSpec((SUBLANES, tm), lambda t, e, c: (0, t)),
            pl.BlockSpec((tm, tm), lambda t, e, c: (0, 0)),
        ],
        out_specs=pl.BlockSpec((tm, d), lambda t, e, c: (t * n_exp + e, 0)),
        scratch_shapes=[pltpu.VMEM((tm, d), BF16), pltpu.VMEM((SUBLANES, tm), F32)],
    )
    return pl.pallas_call(
        _pack_kernel,
        grid_spec=grid_spec,
        out_shape=jax.ShapeDtypeStruct((n * n_exp, d), BF16),
        compiler_params=_cparams(("arbitrary", "arbitrary")),
        name="moe_pack",
    )(counts, x, cwt, tri)


def _unpack_kernel(spos_ref, se_ref, sk_ref, x_ref, cwt_ref, tri_ref, g_ref, b_ref, *rest, alpha, nslot):
    y_refs = rest[:nslot]
    o_ref, pos_ref = rest[nslot:]
    t = pl.program_id(0)
    tm = x_ref.shape[0]
    pos_ref[...] = _slot_rows(cwt_ref, tri_ref)
    sub = lax.broadcasted_iota(jnp.int32, (ROUTE_CHUNK, tm), 0).astype(F32)
    acc = alpha * x_ref[...]
    for s in range(nslot):
        e = se_ref[t, s]
        ee = jnp.maximum(e, 0)
        lo = (sk_ref[t, s] * ROUTE_CHUNK).astype(F32)
        row = pos_ref[pl.ds(ee, 1), :]
        w = jnp.where(e >= 0, cwt_ref[pl.ds(ee, 1), :], 0.0)
        sel = jnp.where(row - lo == sub, w, 0.0)
        acc = acc + _mm(jnp.transpose(sel).astype(BF16), y_refs[s][...])
    o_ref[...] = _layer_norm(acc, g_ref[...], b_ref[...])


def _unpack(x, y, cwt, tri, slot_pos, slot_e, slot_k, g, b, alpha, tm):
    n, d = x.shape
    nslot = slot_pos.shape[1]

    def y_spec(s):
        return pl.BlockSpec((ROUTE_CHUNK, d), lambda t, sp, se, sk: (sp[t, s], 0))

    grid_spec = pltpu.PrefetchScalarGridSpec(
        num_scalar_prefetch=3,
        grid=(n // tm,),
        in_specs=[
            pl.BlockSpec((tm, d), lambda t, sp, se, sk: (t, 0)),
            pl.BlockSpec((SUBLANES, tm), lambda t, sp, se, sk: (0, t)),
            pl.BlockSpec((tm, tm), lambda t, sp, se, sk: (0, 0)),
            pl.BlockSpec((1, d), lambda t, sp, se, sk: (0, 0)),
            pl.BlockSpec((1, d), lambda t, sp, se, sk: (0, 0)),
        ] + [y_spec(s) for s in range(nslot)],
        out_specs=pl.BlockSpec((tm, d), lambda t, sp, se, sk: (t, 0)),
        scratch_shapes=[pltpu.VMEM((SUBLANES, tm), F32)],
    )
    return pl.pallas_call(
        functools.partial(_unpack_kernel, alpha=alpha, nslot=nslot),
        grid_spec=grid_spec,
        out_shape=jax.ShapeDtypeStruct((n, d), F32),
        compiler_params=_cparams(("arbitrary",)),
        name="moe_unpack",
    )(slot_pos, slot_e, slot_k, x, cwt, tri, g, b, *([y] * nslot))


def _residual_ln_kernel(x_ref, y_ref, g_ref, b_ref, o_ref, *, alpha):
    o_ref[...] = _layer_norm(alpha * x_ref[...] + y_ref[...], g_ref[...], b_ref[...])


def _residual_ln(x, y, g, b, alpha, tm):
    n, d = x.shape
    row = pl.BlockSpec((tm, d), lambda i: (i, 0))
    vec = pl.BlockSpec((1, d), lambda i: (0, 0))
    return pl.pallas_call(
        functools.partial(_residual_ln_kernel, alpha=alpha),
        grid=(n // tm,),
        in_specs=[row, row, vec, vec],
        out_specs=row,
        out_shape=jax.ShapeDtypeStruct((n, d), F32),
        compiler_params=_cparams(("arbitrary",)),
        name="residual_ln",
    )(x, y, g, b)


def _rope_tables(pos):
    inv = jnp.power(ROPE_THETA, -jnp.arange(0, HEAD_DIM, 2, dtype=F32) / HEAD_DIM)
    ang = pos.astype(F32)[:, None] * inv[None, :]
    return jnp.cos(ang), jnp.sin(ang)


def _row_tables(cos, sin, nh):
    cq = jnp.tile(jnp.concatenate([cos, cos], axis=1), (1, nh))
    sq = jnp.tile(jnp.concatenate([-sin, sin], axis=1), (1, nh))
    return cq, sq


def _upper_ones(n):
    return (jnp.arange(n)[:, None] <= jnp.arange(n)[None, :]).astype(BF16)


def _route_plan(counts, tm, group):
    nt, n_exp = counts.shape
    kmax = tm // ROUTE_CHUNK
    nslot = n_exp + TOP_K * kmax
    nch = (counts + ROUTE_CHUNK - 1) // ROUTE_CHUNK
    tot = jnp.sum(nch, axis=0)
    tot_pad = (tot + group - 1) // group * group
    off = jnp.cumsum(tot_pad) - tot_pad
    before = jnp.cumsum(nch, axis=0) - nch
    k = jnp.arange(kmax, dtype=jnp.int32)
    lpos = off[None, :, None] + before[:, :, None] + k[None, None, :]
    valid = k[None, None, :] < nch[:, :, None]
    t_i = jnp.arange(nt, dtype=jnp.int32)[:, None, None]
    e_i = jnp.arange(n_exp, dtype=jnp.int32)[None, :, None]
    src = (t_i * n_exp + e_i) * kmax + k[None, None, :]
    lmax = (nt * nslot + n_exp * (group - 1) + group - 1) // group * group
    src_list = jnp.zeros((lmax,), jnp.int32).at[jnp.where(valid, lpos, lmax).reshape(-1)].set(
        src.reshape(-1).astype(jnp.int32), mode="drop")
    step0 = jnp.arange(lmax // group, dtype=jnp.int32) * group
    ends = jnp.cumsum(tot_pad)
    eid = jnp.minimum(jnp.sum(step0[:, None] >= ends[None, :], axis=1), n_exp - 1).astype(jnp.int32)
    nsteps = (jnp.sum(tot_pad) // group).astype(jnp.int32).reshape(1)
    sidx = (jnp.cumsum(nch, axis=1) - nch)[:, :, None] + k[None, None, :]
    sidx = jnp.where(valid, sidx, nslot)
    rows = jnp.broadcast_to(t_i, sidx.shape).reshape(-1)

    def scatter(vals, fill):
        base = jnp.full((nt, nslot), fill, jnp.int32)
        return base.at[rows, sidx.reshape(-1)].set(jnp.broadcast_to(vals, sidx.shape).reshape(-1).astype(jnp.int32),
                                                   mode="drop")
    return src_list, eid, nsteps, scatter(lpos, 0), scatter(e_i, -1), scatter(k[None, None, :], 0)


def _channel_mixer(x, l, alpha, g, b, dense, moe, tm, hi):
    n = x.shape[0]
    nt = n // tm
    i = l // 2
    if l % 2 == 0:
        wg, wu, wd = (w[i:i + 1] for w in dense)
        y = _ffn(x, jnp.arange(nt, dtype=jnp.int32), jnp.zeros((nt,), jnp.int32), jnp.full((1,), nt, jnp.int32),
                 wg, wu, wd, tm, 1, 512, hi, F32)
        return _residual_ln(x, y, g, b, alpha, tm)
    router_w, wg, wu, wd = moe
    n_exp = wg.shape[1]
    tm = min(ROUTE_TILE, n)
    nt = n // tm
    rw = jnp.pad(router_w[i], ((0, 0), (0, LANES - n_exp)))
    cwt, cnt = _router(x, rw, n_exp, tm)
    counts = cnt.reshape(nt, SUBLANES, LANES)[:, 0, 0:n_exp].astype(jnp.int32)
    tri = _upper_ones(tm)
    xc = _pack(x, cwt, counts, tri, tm)
    src, eid, nsteps, slot_pos, slot_e, slot_k = _route_plan(counts, tm, ROUTE_GROUP)
    y = _ffn(xc, src, eid, nsteps, wg[i], wu[i], wd[i], ROUTE_CHUNK, ROUTE_GROUP, 512, False, BF16)
    return _unpack(x, y, cwt, tri, slot_pos, slot_e, slot_k, g, b, alpha, tm)


def kernel(x_prompt, x_sample, cache_moba_k, cache_moba_v, cache_fox_k, cache_fox_v, cache_fox_logf, state_conv,
           page_table, w_in, conv_w, fox_fbias, w_out, ln_mix_g, ln_mix_b, ln_ffn_g, ln_ffn_b, dense_w_gate,
           dense_w_up, dense_w_down, router_w, moe_w_gate, moe_w_up, moe_w_down):
    depth = w_in.shape[0]
    bp, seq, d = x_prompt.shape
    bs, tsz, _ = x_sample.shape
    n_pages = page_table.shape[1]
    page = cache_moba_k.shape[2]
    past = n_pages * page
    alpha = (2.0 * depth) ** 0.25
    assert page == LANES and tsz == SUBLANES and MOBA_BLOCK % page == 0 and past % MOBA_BLOCK == 0
    assert seq % MOBA_BLOCK == 0

    tm_p = min(512, seq)
    npg = 8
    hp = x_prompt.reshape(bp * seq, d)
    hs = x_sample.reshape(bs * tsz, d)

    cos_p, sin_p = _rope_tables(jnp.arange(seq))
    cq_p, sq_p = _row_tables(cos_p, sin_p, MOBA_H)
    ct_p, st_p = cos_p.T, sin_p.T
    cos_s, sin_s = _rope_tables(past + jnp.arange(tsz))
    cq_s, sq_s = _row_tables(jnp.tile(cos_s, (bs, 1)), jnp.tile(sin_s, (bs, 1)), MOBA_H)

    tri_blk, tri_pg, tri_t = _upper_ones(MOBA_BLOCK), _upper_ones(LANES), _upper_ones(tsz)

    def pages_t(c, w):
        return jnp.transpose(c, (0, 1, 3, 4, 2)).reshape(c.shape[0], c.shape[1], w, page)
    cmk, cmv = pages_t(cache_moba_k, MOBA_W), pages_t(cache_moba_v, MOBA_W)
    cfk, cfv = pages_t(cache_fox_k, FOX_W), pages_t(cache_fox_v, FOX_W)
    clf = jnp.transpose(cache_fox_logf, (0, 3, 1, 2))

    dense_b = tuple(w.astype(BF16) for w in (dense_w_gate, dense_w_up, dense_w_down))
    dense_f = (dense_w_gate, dense_w_up, dense_w_down)
    moe_b = (router_w,) + tuple(w.astype(BF16) for w in (moe_w_gate, moe_w_up, moe_w_down))

    rows_p, rows_s = [], []
    for l in range(depth):
        wl = w_in[l]
        fbias = jnp.pad(fox_fbias[l], (0, SUBLANES - FOX_H))
        wo = w_out[l]
        cwl = jnp.pad(conv_w[l], ((0, SUBLANES - CONV_K), (0, 0)))
        g1, b1 = ln_mix_g[l][None], ln_mix_b[l][None]
        g2, b2 = ln_ffn_g[l][None], ln_ffn_b[l][None]

        w_row = jnp.concatenate([wl[:, 0:C_MK], wl[:, C_FQ:C_FK]], axis=1).astype(BF16)
        w_col = jnp.concatenate([wl[:, C_MK:C_FQ], wl[:, C_FK:C_FG],
                                 jnp.pad(wl[:, C_FG:], ((0, 0), (0, SUBLANES - FOX_H)))], axis=1).T.astype(BF16)
        r, tmaj, lf = _inproj_prompt(hp, w_row, w_col, fbias[:, None], cq_p, sq_p, ct_p, st_p, bp, seq, tm_p)
        mo = _attn_prompt(r, tmaj, None, None, bp, seq, False)
        fo = _attn_prompt(r, tmaj, lf, tri_blk, bp, seq, True)
        hp1, utail = _outproj(hp, r, 0, mo, fo, wo.astype(BF16), cwl, g1, b1, alpha, tm_p, seq)
        hp = _channel_mixer(hp1, l, alpha, g2, b2, dense_b, moe_b, tm_p, False)
        nts = seq // tm_p
        conv_new = utail.reshape(bp, nts, SUBLANES, CONV_W)[:, nts - 1, SUBLANES - (CONV_K - 1):]
        kv = tmaj.reshape(bp, 4, MOBA_H, HEAD_DIM, seq)
        kv = jnp.transpose(kv, (1, 0, 4, 2, 3))
        rows_p.append((kv[0], kv[1], kv[2], kv[3], jnp.transpose(lf[:, 0:FOX_H], (0, 2, 1)), conv_new))

        w_s = jnp.pad(wl, ((0, 0), (0, C_FG + MOBA_W - wl.shape[1])))
        fb_row = jnp.pad(fox_fbias[l], (0, MOBA_W - FOX_H))[None]
        zs = _inproj_sample(hs, w_s, fb_row, cq_s, sq_s)
        lfs = zs[:, C_FG:C_FG + LANES]
        lfn = jnp.transpose(lfs[:, 0:SUBLANES].reshape(bs, tsz, SUBLANES), (0, 2, 1))
        mo_s = _moba_decode(page_table, zs, cmk, cmv, l, npg)
        fo_s = _fox_decode(page_table, zs, lfn, cfk, cfv, clf, tri_pg, tri_t, l, npg)
        st = state_conv[l]
        hist = (jnp.repeat(st[:, 0], tsz, axis=0), jnp.repeat(st[:, 1], tsz, axis=0))
        hs1, us = _outproj(hs, zs, 0, mo_s, fo_s, wo, cwl, g1, b1, alpha, bs * tsz, tsz, hist)
        hs = _channel_mixer(hs1, l, alpha, g2, b2, dense_f, moe_b, bs * tsz, True)
        u_ext = jnp.concatenate([st, us.reshape(bs, tsz, CONV_W)], axis=1)
        heads = lambda c0: zs[:, c0:c0 + MOBA_W].reshape(bs, tsz, MOBA_H, HEAD_DIM)
        rows_s.append((heads(C_MK), heads(C_MV), heads(C_FK), heads(C_FV),
                       lfs[:, 0:FOX_H].reshape(bs, tsz, FOX_H), u_ext[:, tsz:]))

    stack = lambda rows, k: jnp.stack([r[k] for r in rows])
    return (hp.reshape(bp, seq, d), hs.reshape(bs, tsz, d),
            *(stack(rows_p, k) for k in range(6)), *(stack(rows_s, k) for k in range(6)))
```

```python
import functools

import jax
import jax.numpy as jnp
from jax import lax
from jax.experimental import pallas as pl
from jax.experimental.pallas import tpu as pltpu

F32 = jnp.float32
BF16 = jnp.bfloat16

HEAD_DIM = 64
HALF = HEAD_DIM // 2
CONV_W = 256
CONV_K = 3
MOBA_H = 6
FOX_H = 6
MOBA_W = MOBA_H * HEAD_DIM
FOX_W = FOX_H * HEAD_DIM
MOBA_BLOCK = 256
MOBA_TOPK = 3
ROPE_THETA = 10000.0
LN_EPS = 1e-5
TOP_K = 2
LANES = 128
SUBLANES = 8
NEG_BIG = -1e30
VMEM_LIMIT = 56 * 1024 * 1024
ROUTE_CHUNK = 128
ROUTE_GROUP = 4
ROUTE_TILE = 1024

C_CB, C_CC, C_CX = 0, CONV_W, 2 * CONV_W
C_MQ = 3 * CONV_W
C_MK = C_MQ + MOBA_W
C_MV = C_MK + MOBA_W
C_FQ = C_MV + MOBA_W
C_FK = C_FQ + FOX_W
C_FV = C_FK + FOX_W
C_FG = C_FV + FOX_W


def _cparams(sem):
    return pltpu.CompilerParams(dimension_semantics=sem, vmem_limit_bytes=VMEM_LIMIT)


def _split2(a):
    hi = a.astype(BF16)
    lo = (a - hi.astype(F32)).astype(BF16)
    return hi, lo


def _split3(a):
    p0 = a.astype(BF16)
    r = a - p0.astype(F32)
    p1 = r.astype(BF16)
    p2 = (r - p1.astype(F32)).astype(BF16)
    return p0, p1, p2


_NN = (((1,), (0,)), ((), ()))
_NT = (((1,), (1,)), ((), ()))


def _mm(a, b, dims=_NN):
    return lax.dot_general(a, b, dims, preferred_element_type=F32)


def _dot(a, b, hi, dims=_NN):
    if not hi:
        return _mm(a.astype(BF16), b.astype(BF16), dims)
    a0, a1 = _split2(a.astype(F32))
    b0, b1 = _split2(b.astype(F32))
    return _mm(a0, b0, dims) + (_mm(a0, b1, dims) + _mm(a1, b0, dims))


def _dot_exact_rhs(a, b_exact, dims=_NN):
    p0, p1, p2 = _split3(a)
    return _mm(p0, b_exact, dims) + (_mm(p1, b_exact, dims) + _mm(p2, b_exact, dims))


def _log_sigmoid(x):
    return jnp.minimum(x, 0.0) - jnp.log1p(jnp.exp(-jnp.abs(x)))


def _silu(x):
    return x * (1.0 / (1.0 + jnp.exp(-x)))


def _layer_norm(y, g, b):
    mu = jnp.mean(y, axis=-1, keepdims=True)
    d = y - mu
    var = jnp.mean(d * d, axis=-1, keepdims=True)
    return d * lax.rsqrt(var + LN_EPS) * g + b


def _rope_rows(x, cos, sin_signed):
    n = x.shape[-1]
    lane = lax.broadcasted_iota(jnp.int32, x.shape, 1)
    first = (lane & (HEAD_DIM - 1)) < HALF
    rot = jnp.where(first, pltpu.roll(x, n - HALF, 1), pltpu.roll(x, HALF, 1))
    return x * cos + rot * sin_signed


def _top_select(g, n_sel):
    lane = lax.broadcasted_iota(jnp.int32, g.shape, 1).astype(F32)
    sel = jnp.zeros(g.shape, F32)
    for _ in range(n_sel):
        m = jnp.max(g, axis=1, keepdims=True)
        idx = jnp.min(jnp.where(g == m, lane, float(g.shape[1])), axis=1, keepdims=True)
        pick = lane == idx
        sel = jnp.where(pick, jnp.where(m > -jnp.inf, 1.0, sel), sel)
        g = jnp.where(pick, -jnp.inf, g)
    return sel


def _inproj_prompt_kernel(x_ref, wr_ref, wt_ref, fb_ref, cq_ref, sq_ref, ct_ref, st_ref,
                          r_ref, t_ref, lf_ref):
    x = x_ref[...].astype(BF16)
    r = _mm(x, wr_ref[...])
    t = _mm(wt_ref[...], x, _NT)
    nq = 3 * CONV_W
    r_ref[:, 0:nq] = r[:, 0:nq]
    r_ref[:, nq:nq + MOBA_W] = _rope_rows(r[:, nq:nq + MOBA_W], cq_ref[...], sq_ref[...])
    r_ref[:, nq + MOBA_W:] = r[:, nq + MOBA_W:]
    cos = ct_ref[...]
    sin = st_ref[...]
    for h in range(MOBA_H):
        a = h * HEAD_DIM
        x1 = t[a:a + HALF]
        x2 = t[a + HALF:a + HEAD_DIM]
        t_ref[0, a:a + HALF, :] = x1 * cos - x2 * sin
        t_ref[0, a + HALF:a + HEAD_DIM, :] = x2 * cos + x1 * sin
    nt = 2 * MOBA_W + 2 * FOX_W
    t_ref[0, MOBA_W:nt, :] = t[MOBA_W:nt]
    lf_ref[0] = _log_sigmoid(t[nt:nt + SUBLANES] + fb_ref[...])


def _inproj_prompt(x, wr, wt, fb, cosq, sinq, cost, sint, batch, seq, tm):
    n, d = x.shape
    nts = seq // tm
    nr = wr.shape[1]
    ntr = wt.shape[0]
    nt = ntr - SUBLANES
    return pl.pallas_call(
        _inproj_prompt_kernel,
        grid=(n // tm,),
        in_specs=[
            pl.BlockSpec((tm, d), lambda i: (i, 0)),
            pl.BlockSpec((d, nr), lambda i: (0, 0)),
            pl.BlockSpec((ntr, d), lambda i: (0, 0)),
            pl.BlockSpec((SUBLANES, 1), lambda i: (0, 0)),
            pl.BlockSpec((tm, MOBA_W), lambda i: (i % nts, 0)),
            pl.BlockSpec((tm, MOBA_W), lambda i: (i % nts, 0)),
            pl.BlockSpec((HALF, tm), lambda i: (0, i % nts)),
            pl.BlockSpec((HALF, tm), lambda i: (0, i % nts)),
        ],
        out_specs=[
            pl.BlockSpec((tm, nr), lambda i: (i, 0)),
            pl.BlockSpec((1, nt, tm), lambda i: (i // nts, 0, i % nts)),
            pl.BlockSpec((1, SUBLANES, tm), lambda i: (i // nts, 0, i % nts)),
        ],
        out_shape=[
            jax.ShapeDtypeStruct((n, nr), F32),
            jax.ShapeDtypeStruct((batch, nt, seq), F32),
            jax.ShapeDtypeStruct((batch, SUBLANES, seq), F32),
        ],
        compiler_params=_cparams(("arbitrary",)),
        name="inproj_prompt",
    )(x, wr, wt, fb, cosq, sinq, cost, sint)


def _inproj_sample_kernel(x_ref, w_ref, fb_ref, cq_ref, sq_ref, z_ref):
    j = pl.program_id(0)
    z = _dot(x_ref[...], w_ref[...], True)
    roped = _rope_rows(z, cq_ref[...], sq_ref[...])
    gate = _log_sigmoid(z + fb_ref[...])
    is_rope = jnp.logical_or(j == C_MQ // MOBA_W, j == C_MK // MOBA_W)
    z_ref[...] = jnp.where(is_rope, roped, jnp.where(j == C_FG // MOBA_W, gate, z))


def _inproj_sample(x, w, fb, cosq, sinq):
    n, d = x.shape
    tn = MOBA_W
    return pl.pallas_call(
        _inproj_sample_kernel,
        grid=(w.shape[1] // tn,),
        in_specs=[
            pl.BlockSpec((n, d), lambda j: (0, 0)),
            pl.BlockSpec((d, tn), lambda j: (0, j)),
            pl.BlockSpec((1, tn), lambda j: (0, 0)),
            pl.BlockSpec((n, tn), lambda j: (0, 0)),
            pl.BlockSpec((n, tn), lambda j: (0, 0)),
        ],
        out_specs=pl.BlockSpec((n, tn), lambda j: (0, j)),
        out_shape=jax.ShapeDtypeStruct((n, w.shape[1]), F32),
        compiler_params=_cparams(("arbitrary",)),
        name="inproj_sample",
    )(x, w, fb, cosq, sinq)


def _attn_prompt_kernel(*refs, fox, nblk):
    if fox:
        q_ref, kt_ref, vt_ref, lf_ref, tri_ref, o_ref, c_ref = refs
    else:
        q_ref, kt_ref, vt_ref, o_ref, km_ref = refs
    p = pl.program_id(1)
    i = pl.program_id(2)
    tq = q_ref.shape[0]
    bs = MOBA_BLOCK
    lane = lax.broadcasted_iota(jnp.int32, (tq, LANES), 1)
    q = q_ref[...] * (HEAD_DIM ** -0.5)
    qh = (jnp.where(lane < HEAD_DIM, q, 0.0), jnp.where(lane < HEAD_DIM, 0.0, q))
    qb = tuple(v.astype(BF16) for v in qh)

    if fox:
        @pl.when(i == 0)
        def _():
            carry = jnp.zeros((SUBLANES, 1), F32)
            for j in range(nblk):
                blk = _dot_exact_rhs(lf_ref[0, :, j * bs:(j + 1) * bs], tri_ref[...]) + carry
                c_ref[:, j * bs:(j + 1) * bs] = blk
                carry = blk[:, bs - 1:bs]
        sel = None
    else:
        @pl.when(i == 0)
        def _():
            km = jnp.zeros((LANES, LANES), F32)
            kl = lax.broadcasted_iota(jnp.int32, (LANES, LANES), 1)
            for j in range(nblk):
                col = jnp.mean(kt_ref[:, j * bs:(j + 1) * bs], axis=1, keepdims=True)
                km = jnp.where(kl == j, col, km)
            km_ref[...] = km
        sel = []
        for h in range(2):
            g = _dot(qh[h], km_ref[...], True)
            g = jnp.where(lane < i, g, -jnp.inf)
            sel.append(_top_select(g, min(MOBA_TOPK, nblk)))

    def bias_rows(h, off):
        row = c_ref[pl.ds(2 * p + h, 1), pl.ds(off, bs)]
        return -row

    q2 = jnp.concatenate(qb, axis=0)

    def step(off, j, state, diag):
        kt = kt_ref[:, pl.ds(off, bs)].astype(BF16)
        vt = vt_ref[:, pl.ds(off, bs)].astype(BF16)
        s2 = _mm(q2, kt)
        stats, ps = [], []
        for h in range(2):
            m, l, _ = state[h]
            s = s2[h * tq:(h + 1) * tq]
            if fox:
                s = s + bias_rows(h, off)
            else:
                if not diag:
                    on = jnp.max(jnp.where(lane == j, sel[h], 0.0), axis=1, keepdims=True)
                    s = jnp.where(on > 0.0, s, -jnp.inf)
            if diag:
                r_i = lax.broadcasted_iota(jnp.int32, (tq, bs), 0)
                c_i = lax.broadcasted_iota(jnp.int32, (tq, bs), 1)
                s = jnp.where(c_i <= r_i, s, -jnp.inf)
            m_new = jnp.maximum(m, jnp.max(s, axis=1, keepdims=True))
            alpha = jnp.exp(m - m_new)
            pexp = jnp.exp(s - m_new)
            stats.append((m_new, l * alpha + jnp.sum(pexp, axis=1, keepdims=True), alpha))
            ps.append(pexp.astype(BF16))
        pv2 = _mm(jnp.concatenate(ps, axis=0), vt, _NT)
        return tuple((stats[h][0], stats[h][1], state[h][2] * stats[h][2] + pv2[h * tq:(h + 1) * tq])
                     for h in range(2))

    init = tuple((jnp.full((tq, 1), NEG_BIG, F32), jnp.zeros((tq, 1), F32), jnp.zeros((tq, LANES), F32))
                 for _ in range(2))
    state = step(pl.multiple_of(i * bs, bs), i, init, True)

    def body(j, st):
        return step(pl.multiple_of(j * bs, bs), j, st, False)

    state = lax.fori_loop(0, i, body, state)
    o0 = state[0][2] / state[0][1]
    o1 = state[1][2] / state[1][1]
    o_ref[...] = jnp.where(lane < HEAD_DIM, o0, o1)


def _attn_prompt(r, tm_arr, lf, tri, batch, seq, fox):
    n = r.shape[0]
    tq = MOBA_BLOCK
    nq = seq // tq
    npair = (FOX_H if fox else MOBA_H) // 2
    qcol0 = (3 * CONV_W + (MOBA_W if fox else 0)) // LANES
    krow0 = (2 * MOBA_W if fox else 0) // LANES
    vrow0 = krow0 + (FOX_W if fox else MOBA_W) // LANES
    in_specs = [
        pl.BlockSpec((tq, LANES), lambda b, p, i: (b * nq + i, qcol0 + p)),
        pl.BlockSpec((None, LANES, seq), lambda b, p, i: (b, krow0 + p, 0)),
        pl.BlockSpec((None, LANES, seq), lambda b, p, i: (b, vrow0 + p, 0)),
    ]
    args = [r, tm_arr, tm_arr]
    if fox:
        in_specs += [pl.BlockSpec((1, SUBLANES, seq), lambda b, p, i: (b, 0, 0)),
                     pl.BlockSpec((tq, tq), lambda b, p, i: (0, 0))]
        args += [lf, tri]
        scratch = [pltpu.VMEM((SUBLANES, seq), F32)]
    else:
        scratch = [pltpu.VMEM((LANES, LANES), F32)]
    return pl.pallas_call(
        functools.partial(_attn_prompt_kernel, fox=fox, nblk=nq),
        grid=(batch, npair, nq),
        in_specs=in_specs,
        out_specs=pl.BlockSpec((tq, LANES), lambda b, p, i: (b * nq + i, p)),
        out_shape=jax.ShapeDtypeStruct((n, npair * LANES), F32),
        scratch_shapes=scratch,
        compiler_params=_cparams(("arbitrary", "arbitrary", "arbitrary")),
        name="fox_prompt" if fox else "moba_prompt",
    )(*args)


def _block_diag_q(q):
    nh = q.shape[1] // HEAD_DIM
    rows = nh * q.shape[0]
    qt = jnp.concatenate([q] * nh, axis=0)
    r_i = lax.broadcasted_iota(jnp.int32, (rows, q.shape[1]), 0)
    c_i = lax.broadcasted_iota(jnp.int32, (rows, q.shape[1]), 1)
    return jnp.where(r_i // q.shape[0] == c_i // HEAD_DIM, qt, 0.0)


def _diag_heads(o, t):
    nh = o.shape[1] // HEAD_DIM
    c_i = lax.broadcasted_iota(jnp.int32, (t, o.shape[1]), 1)
    out = jnp.zeros((t, o.shape[1]), F32)
    for h in range(nh):
        out = jnp.where(c_i // HEAD_DIM == h, o[h * t:(h + 1) * t], out)
    return out


def _expand_heads(c, t):
    nh = FOX_H
    return jnp.concatenate([jnp.broadcast_to(c[h:h + 1], (t, c.shape[1])) for h in range(nh)], axis=0)


def _softmax_update(state, s, v, dims, hi=False):
    m, l, acc = state
    m_new = jnp.maximum(m, jnp.max(s, axis=1, keepdims=True))
    alpha = jnp.exp(m - m_new)
    pexp = jnp.exp(s - m_new)
    l = l * alpha + jnp.sum(pexp, axis=1, keepdims=True)
    acc = acc * alpha + _dot(pexp, v, hi, dims)
    return m_new, l, acc


def _page_copies(pt_ref, srcs, bufs, sem_ref, layer, seq, chunk, slot, npg):
    copies = []
    for k in range(npg):
        page = pt_ref[seq, chunk * npg + k]
        for j, (src, buf) in enumerate(zip(srcs, bufs)):
            copies.append(pltpu.make_async_copy(src(layer, page), buf.at[slot, k], sem_ref.at[j, slot]))
    return copies


def _fox_decode_kernel(pt_ref, q_ref, kn_ref, vn_ref, lfn_ref, tri_ref, tri8_ref, ck_hbm, cv_hbm, clf_hbm,
                       o_ref, m_ref, l_ref, acc_ref, carry_ref, kcat_ref, vcat_ref, kbuf, vbuf, lfbuf, sem,
                       *, layer, npg, tsz):
    b = pl.program_id(0)
    c = pl.program_id(1)
    nb = pl.num_programs(0)
    nc = pl.num_programs(1)
    rows = FOX_H * tsz
    step = b * nc + c
    slot = step % 2
    srcs = (lambda l, p: ck_hbm.at[l, p], lambda l, p: cv_hbm.at[l, p],
            lambda l, p: clf_hbm.at[l, :, pl.ds(p, 1), :])
    bufs = (kbuf, vbuf, lfbuf)

    @pl.when(step == 0)
    def _():
        for cp in _page_copies(pt_ref, srcs, bufs, sem, layer, b, c, slot, npg):
            cp.start()

    @pl.when(step + 1 < nb * nc)
    def _():
        nxt = step + 1
        for cp in _page_copies(pt_ref, srcs, bufs, sem, layer, nxt // nc, nxt % nc, 1 - slot, npg):
            cp.start()

    @pl.when(c == 0)
    def _():
        m_ref[...] = jnp.full(m_ref.shape, NEG_BIG, F32)
        l_ref[...] = jnp.zeros(l_ref.shape, F32)
        acc_ref[...] = jnp.zeros(acc_ref.shape, F32)
        carry_ref[...] = jnp.zeros(carry_ref.shape, F32)

    qbd = _block_diag_q(q_ref[...] * (HEAD_DIM ** -0.5))
    for cp in _page_copies(pt_ref, srcs, bufs, sem, layer, b, c, slot, npg):
        cp.wait()
    state = (m_ref[:, 0:1], l_ref[:, 0:1], acc_ref[...])
    carry = carry_ref[:, 0:1]
    sub = lax.broadcasted_iota(jnp.int32, (SUBLANES, LANES), 0)
    lfs = []
    for k in range(npg):
        lf = jnp.zeros((SUBLANES, LANES), F32)
        for h in range(FOX_H):
            lf = jnp.where(sub == h, jnp.broadcast_to(lfbuf[slot, k, h], (SUBLANES, LANES)), lf)
        lfs.append(lf)
    cum = _dot_exact_rhs(jnp.concatenate(lfs, axis=0), tri_ref[...])
    cs = []
    for k in range(npg):
        ck = cum[k * SUBLANES:(k + 1) * SUBLANES] + carry
        carry = ck[:, LANES - 1:LANES]
        cs.append(_expand_heads(ck, tsz))
        kcat_ref[:, k * LANES:(k + 1) * LANES] = kbuf[slot, k].astype(BF16)
        vcat_ref[:, k * LANES:(k + 1) * LANES] = vbuf[slot, k].astype(BF16)
    s = _mm(qbd.astype(BF16), kcat_ref[...]) - jnp.concatenate(cs, axis=1)
    m, l, acc = state
    m_new = jnp.maximum(m, jnp.max(s, axis=1, keepdims=True))
    alpha = jnp.exp(m - m_new)
    pexp = jnp.exp(s - m_new)
    pv = _mm(pexp.astype(BF16), vcat_ref[...], _NT)
    state = (m_new, l * alpha + jnp.sum(pexp, axis=1, keepdims=True), acc * alpha + pv)

    @pl.when(c < nc - 1)
    def _():
        m_ref[...] = jnp.broadcast_to(state[0], m_ref.shape)
        l_ref[...] = jnp.broadcast_to(state[1], l_ref.shape)
        acc_ref[...] = state[2]
        carry_ref[...] = jnp.broadcast_to(carry, carry_ref.shape)

    @pl.when(c == nc - 1)
    def _():
        cn = _dot_exact_rhs(lfn_ref[...], tri8_ref[...]) + carry
        s = _dot(qbd, kn_ref[...], True, _NT) - _expand_heads(cn, tsz)
        r_i = lax.broadcasted_iota(jnp.int32, (rows, tsz), 0)
        c_i = lax.broadcasted_iota(jnp.int32, (rows, tsz), 1)
        s = jnp.where(c_i <= r_i % tsz, s, -jnp.inf)
        _, l, acc = _softmax_update(state, s, vn_ref[...], _NN, True)
        o_ref[...] = _diag_heads(acc / l, tsz)


def _fox_decode(pt, zs, lfn, cache_k, cache_v, cache_lf, tri, tri8, layer, npg):
    nseq, npages = pt.shape
    tsz = zs.shape[0] // nseq
    nc = npages // npg
    rows = FOX_H * tsz
    any_spec = pl.BlockSpec(memory_space=pl.ANY)
    in_specs = [
        pl.BlockSpec((tsz, FOX_W), lambda b, c, pt: (b, C_FQ // FOX_W)),
        pl.BlockSpec((tsz, FOX_W), lambda b, c, pt: (b, C_FK // FOX_W)),
        pl.BlockSpec((tsz, FOX_W), lambda b, c, pt: (b, C_FV // FOX_W)),
        pl.BlockSpec((None, SUBLANES, tsz), lambda b, c, pt: (b, 0, 0)),
        pl.BlockSpec((LANES, LANES), lambda b, c, pt: (0, 0)),
        pl.BlockSpec((tsz, tsz), lambda b, c, pt: (0, 0)),
        any_spec, any_spec, any_spec,
    ]
    grid_spec = pltpu.PrefetchScalarGridSpec(
        num_scalar_prefetch=1,
        grid=(nseq, nc),
        in_specs=in_specs,
        out_specs=pl.BlockSpec((tsz, FOX_W), lambda b, c, pt: (b, 0)),
        scratch_shapes=[pltpu.VMEM((rows, LANES), F32), pltpu.VMEM((rows, LANES), F32),
                        pltpu.VMEM((rows, FOX_W), F32), pltpu.VMEM((SUBLANES, LANES), F32),
                        pltpu.VMEM((FOX_W, npg * LANES), BF16), pltpu.VMEM((FOX_W, npg * LANES), BF16),
                        pltpu.VMEM((2, npg, FOX_W, LANES), F32), pltpu.VMEM((2, npg, FOX_W, LANES), F32),
                        pltpu.VMEM((2, npg, FOX_H, 1, LANES), F32), pltpu.SemaphoreType.DMA((3, 2))],
    )
    return pl.pallas_call(
        functools.partial(_fox_decode_kernel, layer=layer, npg=npg, tsz=tsz),
        grid_spec=grid_spec,
        out_shape=jax.ShapeDtypeStruct((nseq * tsz, FOX_W), F32),
        compiler_params=_cparams(("arbitrary", "arbitrary")),
        name="fox_decode",
    )(pt, zs, zs, zs, lfn, tri, tri8, cache_k, cache_v, cache_lf)


def _moba_decode_kernel(pt_ref, q_ref, kn_ref, vn_ref, ck_hbm, cv_hbm, o_ref, s_ref, g_ref, sel_ref, m_ref, l_ref,
                        acc_ref, cat_ref, pbuf, sem, *, layer, npg, tsz):
    b = pl.program_id(0)
    ph = pl.program_id(1)
    c = pl.program_id(2)
    nb = pl.num_programs(0)
    nc = pl.num_programs(2)
    rows = MOBA_H * tsz
    ppb = MOBA_BLOCK // LANES
    nblk = npg // ppb
    step = (b * 2 + ph) * nc + c
    slot = step % 2
    bufs = (pbuf,)
    k_src = (lambda l, p: ck_hbm.at[l, p],)
    v_src = (lambda l, p: cv_hbm.at[l, p],)

    def copies(srcs, st, sl):
        return _page_copies(pt_ref, srcs, bufs, sem, layer, st // (2 * nc), st % nc, sl, npg)

    @pl.when(step == 0)
    def _():
        for cp in copies(k_src, step, slot):
            cp.start()

    nxt = step + 1
    nxt_ph = (nxt // nc) % 2

    @pl.when(jnp.logical_and(nxt < nb * 2 * nc, nxt_ph == 0))
    def _():
        for cp in copies(k_src, nxt, 1 - slot):
            cp.start()

    @pl.when(jnp.logical_and(nxt < nb * 2 * nc, nxt_ph == 1))
    def _():
        for cp in copies(v_src, nxt, 1 - slot):
            cp.start()

    qbd = _block_diag_q(q_ref[...] * (HEAD_DIM ** -0.5)).astype(BF16)
    lane = lax.broadcasted_iota(jnp.int32, (rows, LANES), 1)
    off = pl.multiple_of(c * (npg * LANES), npg * LANES)

    @pl.when(ph == 0)
    def _():
        for cp in copies(k_src, step, slot):
            cp.wait()

        @pl.when(c == 0)
        def _():
            g_ref[...] = jnp.full(g_ref.shape, -jnp.inf, F32)
        g = g_ref[...]
        for k in range(npg):
            cat_ref[:, k * LANES:(k + 1) * LANES] = pbuf[slot, k].astype(BF16)
        s = _mm(qbd, cat_ref[...])
        s_ref[:, pl.ds(off, npg * LANES)] = s
        for jb in range(nblk):
            tot = jnp.sum(s[:, jb * MOBA_BLOCK:(jb + 1) * MOBA_BLOCK], axis=1, keepdims=True)
            g = jnp.where(lane == c * nblk + jb, tot * (1.0 / MOBA_BLOCK), g)
        g_ref[...] = g

    @pl.when(ph == 1)
    def _():
        for cp in copies(v_src, step, slot):
            cp.wait()

        @pl.when(c == 0)
        def _():
            sel_ref[...] = _top_select(g_ref[...], MOBA_TOPK)
            s = _mm(qbd, kn_ref[...].astype(BF16), _NT)
            r_i = lax.broadcasted_iota(jnp.int32, (rows, tsz), 0)
            c_i = lax.broadcasted_iota(jnp.int32, (rows, tsz), 1)
            s = jnp.where(c_i <= r_i % tsz, s, -jnp.inf)
            init = (jnp.full((rows, 1), NEG_BIG, F32), jnp.zeros((rows, 1), F32),
                    jnp.zeros((rows, MOBA_W), F32))
            m, l, acc = _softmax_update(init, s, vn_ref[...], _NN)
            m_ref[...] = jnp.broadcast_to(m, m_ref.shape)
            l_ref[...] = jnp.broadcast_to(l, l_ref.shape)
            acc_ref[...] = acc

        m, l, acc = m_ref[:, 0:1], l_ref[:, 0:1], acc_ref[...]
        sel = sel_ref[...]
        sc = s_ref[:, pl.ds(off, npg * LANES)]
        ss = []
        for jb in range(nblk):
            on = jnp.max(jnp.where(lane == c * nblk + jb, sel, 0.0), axis=1, keepdims=True)
            ss.append(jnp.where(on > 0.0, sc[:, jb * MOBA_BLOCK:(jb + 1) * MOBA_BLOCK], -jnp.inf))
        for k in range(npg):
            cat_ref[:, k * LANES:(k + 1) * LANES] = pbuf[slot, k].astype(BF16)
        s = jnp.concatenate(ss, axis=1)
        m_new = jnp.maximum(m, jnp.max(s, axis=1, keepdims=True))
        alpha = jnp.exp(m - m_new)
        pexp = jnp.exp(s - m_new)
        pv = _mm(pexp.astype(BF16), cat_ref[...], _NT)
        l_new = l * alpha + jnp.sum(pexp, axis=1, keepdims=True)
        acc_new = acc * alpha + pv
        m_ref[...] = jnp.broadcast_to(m_new, m_ref.shape)
        l_ref[...] = jnp.broadcast_to(l_new, l_ref.shape)
        acc_ref[...] = acc_new

        @pl.when(c == nc - 1)
        def _():
            o_ref[...] = _diag_heads(acc_new / l_new, tsz)


def _moba_decode(pt, zs, cache_k, cache_v, layer, npg):
    nseq, npages = pt.shape
    tsz = zs.shape[0] // nseq
    nc = npages // npg
    rows = MOBA_H * tsz
    any_spec = pl.BlockSpec(memory_space=pl.ANY)
    in_specs = [
        pl.BlockSpec((tsz, MOBA_W), lambda b, ph, c, pt: (b, C_MQ // MOBA_W)),
        pl.BlockSpec((tsz, MOBA_W), lambda b, ph, c, pt: (b, C_MK // MOBA_W)),
        pl.BlockSpec((tsz, MOBA_W), lambda b, ph, c, pt: (b, C_MV // MOBA_W)),
        any_spec, any_spec,
    ]
    grid_spec = pltpu.PrefetchScalarGridSpec(
        num_scalar_prefetch=1,
        grid=(nseq, 2, nc),
        in_specs=in_specs,
        out_specs=pl.BlockSpec((tsz, MOBA_W), lambda b, ph, c, pt: (b, 0)),
        scratch_shapes=[pltpu.VMEM((rows, npages * LANES), F32), pltpu.VMEM((rows, LANES), F32),
                        pltpu.VMEM((rows, LANES), F32), pltpu.VMEM((rows, LANES), F32),
                        pltpu.VMEM((rows, LANES), F32), pltpu.VMEM((rows, MOBA_W), F32),
                        pltpu.VMEM((MOBA_W, npg * LANES), BF16),
                        pltpu.VMEM((2, npg, MOBA_W, LANES), F32), pltpu.SemaphoreType.DMA((1, 2))],
    )
    return pl.pallas_call(
        functools.partial(_moba_decode_kernel, layer=layer, npg=npg, tsz=tsz),
        grid_spec=grid_spec,
        out_shape=jax.ShapeDtypeStruct((nseq * tsz, MOBA_W), F32),
        compiler_params=_cparams(("arbitrary", "arbitrary", "arbitrary")),
        name="moba_decode",
    )(pt, zs, zs, zs, cache_k, cache_v)


def _outproj_kernel(*refs, alpha, sample, hi, nts):
    if sample:
        x_ref, cb_ref, cc_ref, cx_ref, h0_ref, h1_ref, mo_ref, fo_ref, wo_ref, cw_ref, g_ref, b_ref, \
            y_ref, u_ref = refs
    else:
        x_ref, cb_ref, cc_ref, cx_ref, cch_ref, cxh_ref, mo_ref, fo_ref, wo_ref, cw_ref, g_ref, b_ref, \
            y_ref, u_ref = refs
    u = cc_ref[...] * cx_ref[...]
    tm = u.shape[0]
    row = lax.broadcasted_iota(jnp.int32, u.shape, 0)
    r1 = pltpu.roll(u, 1, 0)
    r2 = pltpu.roll(u, 2, 0)
    if sample:
        t = row % SUBLANES
        s1 = jnp.where(t == 0, h1_ref[...], r1)
        s2 = jnp.where(t == 0, h0_ref[...], jnp.where(t == 1, h1_ref[...], r2))
        u_ref[...] = u
    else:
        first = pl.program_id(0) % nts == 0
        uh = jnp.where(first, 0.0, cch_ref[...] * cxh_ref[...])
        p1 = jnp.broadcast_to(uh[SUBLANES - 1:SUBLANES], u.shape)
        p2 = jnp.broadcast_to(uh[SUBLANES - 2:SUBLANES - 1], u.shape)
        s1 = jnp.where(row == 0, p1, r1)
        s2 = jnp.where(row == 0, p2, jnp.where(row == 1, p1, r2))
        u_ref[...] = u[tm - SUBLANES:tm]
    cw = cw_ref[...]
    conv = cb_ref[...] * (s2 * cw[0:1] + s1 * cw[1:2] + u * cw[2:3])
    mixed = (_dot(conv, wo_ref[0:CONV_W, :], hi)
             + _dot(mo_ref[...], wo_ref[CONV_W:CONV_W + MOBA_W, :], hi)
             + _dot(fo_ref[...], wo_ref[CONV_W + MOBA_W:, :], hi))
    y_ref[...] = _layer_norm(alpha * x_ref[...] + mixed, g_ref[...], b_ref[...])


def _outproj(x, r, ccol, mo, fo, wo, cw, g, b, alpha, tm, seq, hist=None):
    n, d = x.shape
    sample = hist is not None
    nts = 1 if sample else seq // tm
    hb = tm // SUBLANES
    col = lambda k: pl.BlockSpec((tm, CONV_W), lambda i: (i, ccol + k))
    if sample:
        extra = [pl.BlockSpec((tm, CONV_W), lambda i: (i, 0)), pl.BlockSpec((tm, CONV_W), lambda i: (i, 0))]
        extra_args = list(hist)
        u_shape, u_spec = (n, CONV_W), pl.BlockSpec((tm, CONV_W), lambda i: (i, 0))
    else:
        halo = lambda k: pl.BlockSpec((SUBLANES, CONV_W), lambda i: (jnp.maximum(i * hb - 1, 0), ccol + k))
        extra = [halo(1), halo(2)]
        extra_args = [r, r]
        u_shape, u_spec = (n // tm * SUBLANES, CONV_W), pl.BlockSpec((SUBLANES, CONV_W), lambda i: (i, 0))
    return pl.pallas_call(
        functools.partial(_outproj_kernel, alpha=alpha, sample=sample, hi=sample, nts=nts),
        grid=(n // tm,),
        in_specs=[pl.BlockSpec((tm, d), lambda i: (i, 0)), col(0), col(1), col(2)] + extra + [
            pl.BlockSpec((tm, MOBA_W), lambda i: (i, 0)),
            pl.BlockSpec((tm, FOX_W), lambda i: (i, 0)),
            pl.BlockSpec(wo.shape, lambda i: (0, 0)),
            pl.BlockSpec(cw.shape, lambda i: (0, 0)),
            pl.BlockSpec((1, d), lambda i: (0, 0)),
            pl.BlockSpec((1, d), lambda i: (0, 0)),
        ],
        out_specs=[pl.BlockSpec((tm, d), lambda i: (i, 0)), u_spec],
        out_shape=[jax.ShapeDtypeStruct((n, d), F32), jax.ShapeDtypeStruct(u_shape, F32)],
        compiler_params=_cparams(("arbitrary",)),
        name="outproj_sample" if sample else "outproj_prompt",
    )(x, r, r, r, *extra_args, mo, fo, wo, cw, g, b)


def _ffn_kernel(src_ref, eid_ref, ns_ref, *refs, g, hi):
    x_refs = refs[:g]
    wg_ref, wu_ref, wd_ref, o_ref, acc_ref = refs[g:]
    i = pl.program_id(0)
    f = pl.program_id(1)
    last = f == pl.num_programs(1) - 1
    live = i < ns_ref[0]

    @pl.when(live)
    def _():
        x = x_refs[0][...] if g == 1 else jnp.concatenate([r[...] for r in x_refs], axis=0)
        h = _silu(_dot(x, wg_ref[...], hi)) * _dot(x, wu_ref[...], hi)
        part = _dot(h, wd_ref[...], hi)

        @pl.when(f == 0)
        def _():
            acc_ref[...] = part

        @pl.when(f > 0)
        def _():
            acc_ref[...] = acc_ref[...] + part

        @pl.when(last)
        def _():
            o_ref[...] = acc_ref[...].astype(o_ref.dtype)

    @pl.when(jnp.logical_and(jnp.logical_not(live), last))
    def _():
        o_ref[...] = jnp.zeros(o_ref.shape, o_ref.dtype)


def _ffn(x, src, eid, nsteps, wg, wu, wd, rows, g, tf, hi, out_dtype):
    d = x.shape[1]
    ff = wg.shape[2]
    ns = eid.shape[0]

    def x_spec(j):
        return pl.BlockSpec((rows, d), lambda i, f, src, eid, n: (src[i * g + j], 0))

    grid_spec = pltpu.PrefetchScalarGridSpec(
        num_scalar_prefetch=3,
        grid=(ns, ff // tf),
        in_specs=[x_spec(j) for j in range(g)] + [
            pl.BlockSpec((None, d, tf), lambda i, f, src, eid, n: (eid[i], 0, f)),
            pl.BlockSpec((None, d, tf), lambda i, f, src, eid, n: (eid[i], 0, f)),
            pl.BlockSpec((None, tf, d), lambda i, f, src, eid, n: (eid[i], f, 0)),
        ],
        out_specs=pl.BlockSpec((g * rows, d), lambda i, f, src, eid, n: (i, 0)),
        scratch_shapes=[pltpu.VMEM((g * rows, d), F32)],
    )
    return pl.pallas_call(
        functools.partial(_ffn_kernel, g=g, hi=hi),
        grid_spec=grid_spec,
        out_shape=jax.ShapeDtypeStruct((ns * g * rows, d), out_dtype),
        compiler_params=_cparams(("arbitrary", "arbitrary")),
        name="ffn_hi" if hi else "ffn",
    )(src, eid, nsteps, *([x] * g), wg, wu, wd)


def _router_kernel(x_ref, w_ref, cwt_ref, cnt_ref, *, n_exp):
    logits = _dot(x_ref[...], w_ref[...], True)
    lane = lax.broadcasted_iota(jnp.int32, logits.shape, 1)
    logits = jnp.where(lane < n_exp, logits, -jnp.inf)
    m = jnp.max(logits, axis=1, keepdims=True)
    e = jnp.exp(logits - m)
    probs = e / jnp.sum(e, axis=1, keepdims=True)
    sel = _top_select(jnp.where(lane < n_exp, probs, -jnp.inf), TOP_K)
    top = probs * sel
    cw = top / jnp.sum(top, axis=1, keepdims=True)
    cwt_ref[...] = jnp.transpose(cw)[0:SUBLANES]
    cnt_ref[...] = jnp.broadcast_to(jnp.sum(sel, axis=0, keepdims=True), cnt_ref.shape)


def _router(x, w, n_exp, tm):
    n, d = x.shape
    return pl.pallas_call(
        functools.partial(_router_kernel, n_exp=n_exp),
        grid=(n // tm,),
        in_specs=[pl.BlockSpec((tm, d), lambda i: (i, 0)), pl.BlockSpec((d, LANES), lambda i: (0, 0))],
        out_specs=[pl.BlockSpec((SUBLANES, tm), lambda i: (0, i)),
                   pl.BlockSpec((SUBLANES, LANES), lambda i: (i, 0))],
        out_shape=[jax.ShapeDtypeStruct((SUBLANES, n), F32),
                   jax.ShapeDtypeStruct((n // tm * SUBLANES, LANES), F32)],
        compiler_params=_cparams(("arbitrary",)),
        name="router",
    )(x, w)


def _slot_rows(cwt_ref, tri_ref):
    on = cwt_ref[...] > 0.0
    cum = _mm(jnp.where(on, 1.0, 0.0).astype(BF16), tri_ref[...])
    return jnp.where(on, cum - 1.0, -1.0)


def _pack_kernel(cnt_ref, x_ref, cwt_ref, tri_ref, xc_ref, xb_ref, pos_ref):
    t = pl.program_id(0)
    e = pl.program_id(1)
    tm = x_ref.shape[0]

    @pl.when(e == 0)
    def _():
        xb_ref[...] = x_ref[...].astype(BF16)
        pos_ref[...] = _slot_rows(cwt_ref, tri_ref)

    row = pos_ref[pl.ds(e, 1), :]
    sub = lax.broadcasted_iota(jnp.int32, (ROUTE_CHUNK, tm), 0).astype(F32)
    for k in range(tm // ROUTE_CHUNK):
        lo = k * ROUTE_CHUNK

        @pl.when(cnt_ref[t, e] > lo)
        def _():
            sel = jnp.where(row - float(lo) == sub, 1.0, 0.0).astype(BF16)
            xc_ref[lo:lo + ROUTE_CHUNK, :] = _mm(sel, xb_ref[...]).astype(BF16)

        @pl.when(cnt_ref[t, e] <= lo)
        def _():
            xc_ref[lo:lo + ROUTE_CHUNK, :] = jnp.zeros((ROUTE_CHUNK, xc_ref.shape[1]), BF16)


def _pack(x, cwt, counts, tri, tm):
    n, d = x.shape
    n_exp = counts.shape[1]
    grid_spec = pltpu.PrefetchScalarGridSpec(
        num_scalar_prefetch=1,
        grid=(n // tm, n_exp),
        in_specs=[
            pl.BlockSpec((tm, d), lambda t, e, c: (t, 0)),
            pl.BlockSpec((SUBLANES, tm), lambda t, e, c: (0, t)),
            pl.BlockSpec((tm, tm), lambda t, e, c: (0, 0)),
        ],
        out_specs=pl.BlockSpec((tm, d), lambda t, e, c: (t * n_exp + e, 0)),
        scratch_shapes=[pltpu.VMEM((tm, d), BF16), pltpu.VMEM((SUBLANES, tm), F32)],
    )
    return pl.pallas_call(
        _pack_kernel,
        grid_spec=grid_spec,
        out_shape=jax.ShapeDtypeStruct((n * n_exp, d), BF16),
        compiler_params=_cparams(("arbitrary", "arbitrary")),
        name="moe_pack",
    )(counts, x, cwt, tri)


def _unpack_kernel(spos_ref, se_ref, sk_ref, x_ref, cwt_ref, tri_ref, g_ref, b_ref, *rest, alpha, nslot):
    y_refs = rest[:nslot]
    o_ref, pos_ref = rest[nslot:]
    t = pl.program_id(0)
    tm = x_ref.shape[0]
    pos_ref[...] = _slot_rows(cwt_ref, tri_ref)
    sub = lax.broadcasted_iota(jnp.int32, (ROUTE_CHUNK, tm), 0).astype(F32)
    acc = alpha * x_ref[...]
    for s in range(nslot):
        e = se_ref[t, s]
        ee = jnp.maximum(e, 0)
        lo = (sk_ref[t, s] * ROUTE_CHUNK).astype(F32)
        row = pos_ref[pl.ds(ee, 1), :]
        w = jnp.where(e >= 0, cwt_ref[pl.ds(ee, 1), :], 0.0)
        sel = jnp.where(row - lo == sub, w, 0.0)
        acc = acc + _mm(jnp.transpose(sel).astype(BF16), y_refs[s][...])
    o_ref[...] = _layer_norm(acc, g_ref[...], b_ref[...])


def _unpack(x, y, cwt, tri, slot_pos, slot_e, slot_k, g, b, alpha, tm):
    n, d = x.shape
    nslot = slot_pos.shape[1]

    def y_spec(s):
        return pl.BlockSpec((ROUTE_CHUNK, d), lambda t, sp, se, sk: (sp[t, s], 0))

    grid_spec = pltpu.PrefetchScalarGridSpec(
        num_scalar_prefetch=3,
        grid=(n // tm,),
        in_specs=[
            pl.BlockSpec((tm, d), lambda t, sp, se, sk: (t, 0)),
            pl.BlockSpec((SUBLANES, tm), lambda t, sp, se, sk: (0, t)),
            pl.BlockSpec((tm, tm), lambda t, sp, se, sk: (0, 0)),
            pl.BlockSpec((1, d), lambda t, sp, se, sk: (0, 0)),
            pl.BlockSpec((1, d), lambda t, sp, se, sk: (0, 0)),
        ] + [y_spec(s) for s in range(nslot)],
        out_specs=pl.BlockSpec((tm, d), lambda t, sp, se, sk: (t, 0)),
        scratch_shapes=[pltpu.VMEM((SUBLANES, tm), F32)],
    )
    return pl.pallas_call(
        functools.partial(_unpack_kernel, alpha=alpha, nslot=nslot),
        grid_spec=grid_spec,
        out_shape=jax.ShapeDtypeStruct((n, d), F32),
        compiler_params=_cparams(("arbitrary",)),
        name="moe_unpack",
    )(slot_pos, slot_e, slot_k, x, cwt, tri, g, b, *([y] * nslot))


def _residual_ln_kernel(x_ref, y_ref, g_ref, b_ref, o_ref, *, alpha):
    o_ref[...] = _layer_norm(alpha * x_ref[...] + y_ref[...], g_ref[...], b_ref[...])


def _residual_ln(x, y, g, b, alpha, tm):
    n, d = x.shape
    row = pl.BlockSpec((tm, d), lambda i: (i, 0))
    vec = pl.BlockSpec((1, d), lambda i: (0, 0))
    return pl.pallas_call(
        functools.partial(_residual_ln_kernel, alpha=alpha),
        grid=(n // tm,),
        in_specs=[row, row, vec, vec],
        out_specs=row,
        out_shape=jax.ShapeDtypeStruct((n, d), F32),
        compiler_params=_cparams(("arbitrary",)),
        name="residual_ln",
    )(x, y, g, b)


def _rope_tables(pos):
    inv = jnp.power(ROPE_THETA, -jnp.arange(0, HEAD_DIM, 2, dtype=F32) / HEAD_DIM)
    ang = pos.astype(F32)[:, None] * inv[None, :]
    return jnp.cos(ang), jnp.sin(ang)


def _row_tables(cos, sin, nh):
    cq = jnp.tile(jnp.concatenate([cos, cos], axis=1), (1, nh))
    sq = jnp.tile(jnp.concatenate([-sin, sin], axis=1), (1, nh))
    return cq, sq


def _upper_ones(n):
    return (jnp.arange(n)[:, None] <= jnp.arange(n)[None, :]).astype(BF16)


def _route_plan(counts, tm, group):
    nt, n_exp = counts.shape
    kmax = tm // ROUTE_CHUNK
    nslot = n_exp + TOP_K * kmax
    nch = (counts + ROUTE_CHUNK - 1) // ROUTE_CHUNK
    tot = jnp.sum(nch, axis=0)
    tot_pad = (tot + group - 1) // group * group
    off = jnp.cumsum(tot_pad) - tot_pad
    before = jnp.cumsum(nch, axis=0) - nch
    k = jnp.arange(kmax, dtype=jnp.int32)
    lpos = off[None, :, None] + before[:, :, None] + k[None, None, :]
    valid = k[None, None, :] < nch[:, :, None]
    t_i = jnp.arange(nt, dtype=jnp.int32)[:, None, None]
    e_i = jnp.arange(n_exp, dtype=jnp.int32)[None, :, None]
    src = (t_i * n_exp + e_i) * kmax + k[None, None, :]
    lmax = (nt * nslot + n_exp * (group - 1) + group - 1) // group * group
    src_list = jnp.zeros((lmax,), jnp.int32).at[jnp.where(valid, lpos, lmax).reshape(-1)].set(
        src.reshape(-1).astype(jnp.int32), mode="drop")
    step0 = jnp.arange(lmax // group, dtype=jnp.int32) * group
    ends = jnp.cumsum(tot_pad)
    eid = jnp.minimum(jnp.sum(step0[:, None] >= ends[None, :], axis=1), n_exp - 1).astype(jnp.int32)
    nsteps = (jnp.sum(tot_pad) // group).astype(jnp.int32).reshape(1)
    sidx = (jnp.cumsum(nch, axis=1) - nch)[:, :, None] + k[None, None, :]
    sidx = jnp.where(valid, sidx, nslot)
    rows = jnp.broadcast_to(t_i, sidx.shape).reshape(-1)

    def scatter(vals, fill):
        base = jnp.full((nt, nslot), fill, jnp.int32)
        return base.at[rows, sidx.reshape(-1)].set(jnp.broadcast_to(vals, sidx.shape).reshape(-1).astype(jnp.int32),
                                                   mode="drop")
    return src_list, eid, nsteps, scatter(lpos, 0), scatter(e_i, -1), scatter(k[None, None, :], 0)


def _channel_mixer(x, l, alpha, g, b, dense, moe, tm, hi):
    n = x.shape[0]
    nt = n // tm
    i = l // 2
    if l % 2 == 0:
        wg, wu, wd = (w[i:i + 1] for w in dense)
        y = _ffn(x, jnp.arange(nt, dtype=jnp.int32), jnp.zeros((nt,), jnp.int32), jnp.full((1,), nt, jnp.int32),
                 wg, wu, wd, tm, 1, 512, hi, F32)
        return _residual_ln(x, y, g, b, alpha, tm)
    router_w, wg, wu, wd = moe
    n_exp = wg.shape[1]
    tm = min(ROUTE_TILE, n)
    nt = n // tm
    rw = jnp.pad(router_w[i], ((0, 0), (0, LANES - n_exp)))
    cwt, cnt = _router(x, rw, n_exp, tm)
    counts = cnt.reshape(nt, SUBLANES, LANES)[:, 0, 0:n_exp].astype(jnp.int32)
    tri = _upper_ones(tm)
    xc = _pack(x, cwt, counts, tri, tm)
    src, eid, nsteps, slot_pos, slot_e, slot_k = _route_plan(counts, tm, ROUTE_GROUP)
    y = _ffn(xc, src, eid, nsteps, wg[i], wu[i], wd[i], ROUTE_CHUNK, ROUTE_GROUP, 512, False, BF16)
    return _unpack(x, y, cwt, tri, slot_pos, slot_e, slot_k, g, b, alpha, tm)


def kernel(x_prompt, x_sample, cache_moba_k, cache_moba_v, cache_fox_k, cache_fox_v, cache_fox_logf, state_conv,
           page_table, w_in, conv_w, fox_fbias, w_out, ln_mix_g, ln_mix_b, ln_ffn_g, ln_ffn_b, dense_w_gate,
           dense_w_up, dense_w_down, router_w, moe_w_gate, moe_w_up, moe_w_down):
    depth = w_in.shape[0]
    bp, seq, d = x_prompt.shape
    bs, tsz, _ = x_sample.shape
    n_pages = page_table.shape[1]
    page = cache_moba_k.shape[2]
    past = n_pages * page
    alpha = (2.0 * depth) ** 0.25
    assert page == LANES and tsz == SUBLANES and MOBA_BLOCK % page == 0 and past % MOBA_BLOCK == 0
    assert seq % MOBA_BLOCK == 0

    tm_p = min(512, seq)
    npg = 16
    hp = x_prompt.reshape(bp * seq, d)
    hs = x_sample.reshape(bs * tsz, d)

    cos_p, sin_p = _rope_tables(jnp.arange(seq))
    cq_p, sq_p = _row_tables(cos_p, sin_p, MOBA_H)
    ct_p, st_p = cos_p.T, sin_p.T
    cos_s, sin_s = _rope_tables(past + jnp.arange(tsz))
    cq_s, sq_s = _row_tables(jnp.tile(cos_s, (bs, 1)), jnp.tile(sin_s, (bs, 1)), MOBA_H)

    tri_blk, tri_pg, tri_t = _upper_ones(MOBA_BLOCK), _upper_ones(LANES), _upper_ones(tsz)

    def pages_t(c, w):
        return jnp.transpose(c, (0, 1, 3, 4, 2)).reshape(c.shape[0], c.shape[1], w, page)
    cmk, cmv = pages_t(cache_moba_k, MOBA_W), pages_t(cache_moba_v, MOBA_W)
    cfk, cfv = pages_t(cache_fox_k, FOX_W), pages_t(cache_fox_v, FOX_W)
    clf = jnp.transpose(cache_fox_logf, (0, 3, 1, 2))

    dense_b = tuple(w.astype(BF16) for w in (dense_w_gate, dense_w_up, dense_w_down))
    dense_f = (dense_w_gate, dense_w_up, dense_w_down)
    moe_b = (router_w,) + tuple(w.astype(BF16) for w in (moe_w_gate, moe_w_up, moe_w_down))

    rows_p, rows_s = [], []
    for l in range(depth):
        wl = w_in[l]
        fbias = jnp.pad(fox_fbias[l], (0, SUBLANES - FOX_H))
        wo = w_out[l]
        cwl = jnp.pad(conv_w[l], ((0, SUBLANES - CONV_K), (0, 0)))
        g1, b1 = ln_mix_g[l][None], ln_mix_b[l][None]
        g2, b2 = ln_ffn_g[l][None], ln_ffn_b[l][None]

        w_row = jnp.concatenate([wl[:, 0:C_MK], wl[:, C_FQ:C_FK]], axis=1).astype(BF16)
        w_col = jnp.concatenate([wl[:, C_MK:C_FQ], wl[:, C_FK:C_FG],
                                 jnp.pad(wl[:, C_FG:], ((0, 0), (0, SUBLANES - FOX_H)))], axis=1).T.astype(BF16)
        r, tmaj, lf = _inproj_prompt(hp, w_row, w_col, fbias[:, None], cq_p, sq_p, ct_p, st_p, bp, seq, tm_p)
        mo = _attn_prompt(r, tmaj, None, None, bp, seq, False)
        fo = _attn_prompt(r, tmaj, lf, tri_blk, bp, seq, True)
        hp1, utail = _outproj(hp, r, 0, mo, fo, wo.astype(BF16), cwl, g1, b1, alpha, tm_p, seq)
        hp = _channel_mixer(hp1, l, alpha, g2, b2, dense_b, moe_b, tm_p, False)
        nts = seq // tm_p
        conv_new = utail.reshape(bp, nts, SUBLANES, CONV_W)[:, nts - 1, SUBLANES - (CONV_K - 1):]
        kv = tmaj.reshape(bp, 4, MOBA_H, HEAD_DIM, seq)
        kv = jnp.transpose(kv, (1, 0, 4, 2, 3))
        rows_p.append((kv[0], kv[1], kv[2], kv[3], jnp.transpose(lf[:, 0:FOX_H], (0, 2, 1)), conv_new))

        w_s = jnp.pad(wl, ((0, 0), (0, C_FG + MOBA_W - wl.shape[1])))
        fb_row = jnp.pad(fox_fbias[l], (0, MOBA_W - FOX_H))[None]
        zs = _inproj_sample(hs, w_s, fb_row, cq_s, sq_s)
        lfs = zs[:, C_FG:C_FG + LANES]
        lfn = jnp.transpose(lfs[:, 0:SUBLANES].reshape(bs, tsz, SUBLANES), (0, 2, 1))
        mo_s = _moba_decode(page_table, zs, cmk, cmv, l, npg)
        fo_s = _fox_decode(page_table, zs, lfn, cfk, cfv, clf, tri_pg, tri_t, l, npg)
        st = state_conv[l]
        hist = (jnp.repeat(st[:, 0], tsz, axis=0), jnp.repeat(st[:, 1], tsz, axis=0))
        hs1, us = _outproj(hs, zs, 0, mo_s, fo_s, wo, cwl, g1, b1, alpha, bs * tsz, tsz, hist)
        hs = _channel_mixer(hs1, l, alpha, g2, b2, dense_f, moe_b, bs * tsz, True)
        u_ext = jnp.concatenate([st, us.reshape(bs, tsz, CONV_W)], axis=1)
        heads = lambda c0: zs[:, c0:c0 + MOBA_W].reshape(bs, tsz, MOBA_H, HEAD_DIM)
        rows_s.append((heads(C_MK), heads(C_MV), heads(C_FK), heads(C_FV),
                       lfs[:, 0:FOX_H].reshape(bs, tsz, FOX_H), u_ext[:, tsz:]))

    stack = lambda rows, k: jnp.stack([r[k] for r in rows])
    return (hp.reshape(bp, seq, d), hs.reshape(bs, tsz, d),
            *(stack(rows_p, k) for k in range(6)), *(stack(rows_s, k) for k in range(6)))
```

```python
import functools

import jax
import jax.numpy as jnp
from jax import lax
from jax.experimental import pallas as pl
from jax.experimental.pallas import tpu as pltpu

F32 = jnp.float32
BF16 = jnp.bfloat16

HEAD_DIM = 64
HALF = HEAD_DIM // 2
CONV_W = 256
CONV_K = 3
MOBA_H = 6
FOX_H = 6
MOBA_W = MOBA_H * HEAD_DIM
FOX_W = FOX_H * HEAD_DIM
MOBA_BLOCK = 256
MOBA_TOPK = 3
ROPE_THETA = 10000.0
LN_EPS = 1e-5
TOP_K = 2
LANES = 128
SUBLANES = 8
NEG_BIG = -1e30
VMEM_LIMIT = 56 * 1024 * 1024
ROUTE_CHUNK = 128
ROUTE_GROUP = 8
ROUTE_TILE = 1024

C_CB, C_CC, C_CX = 0, CONV_W, 2 * CONV_W
C_MQ = 3 * CONV_W
C_MK = C_MQ + MOBA_W
C_MV = C_MK + MOBA_W
C_FQ = C_MV + MOBA_W
C_FK = C_FQ + FOX_W
C_FV = C_FK + FOX_W
C_FG = C_FV + FOX_W


def _cparams(sem):
    return pltpu.CompilerParams(dimension_semantics=sem, vmem_limit_bytes=VMEM_LIMIT)


def _split2(a):
    hi = a.astype(BF16)
    lo = (a - hi.astype(F32)).astype(BF16)
    return hi, lo


def _split3(a):
    p0 = a.astype(BF16)
    r = a - p0.astype(F32)
    p1 = r.astype(BF16)
    p2 = (r - p1.astype(F32)).astype(BF16)
    return p0, p1, p2


_NN = (((1,), (0,)), ((), ()))
_NT = (((1,), (1,)), ((), ()))


def _mm(a, b, dims=_NN):
    return lax.dot_general(a, b, dims, preferred_element_type=F32)


def _dot(a, b, hi, dims=_NN):
    if not hi:
        return _mm(a.astype(BF16), b.astype(BF16), dims)
    a0, a1 = _split2(a.astype(F32))
    b0, b1 = _split2(b.astype(F32))
    return _mm(a0, b0, dims) + (_mm(a0, b1, dims) + _mm(a1, b0, dims))


def _dot_exact_rhs(a, b_exact, dims=_NN):
    p0, p1, p2 = _split3(a)
    return _mm(p0, b_exact, dims) + (_mm(p1, b_exact, dims) + _mm(p2, b_exact, dims))


def _log_sigmoid(x):
    return jnp.minimum(x, 0.0) - jnp.log1p(jnp.exp(-jnp.abs(x)))


def _silu(x):
    return x * (1.0 / (1.0 + jnp.exp(-x)))


def _layer_norm(y, g, b):
    mu = jnp.mean(y, axis=-1, keepdims=True)
    d = y - mu
    var = jnp.mean(d * d, axis=-1, keepdims=True)
    return d * lax.rsqrt(var + LN_EPS) * g + b


def _rope_rows(x, cos, sin_signed):
    n = x.shape[-1]
    lane = lax.broadcasted_iota(jnp.int32, x.shape, 1)
    first = (lane & (HEAD_DIM - 1)) < HALF
    rot = jnp.where(first, pltpu.roll(x, n - HALF, 1), pltpu.roll(x, HALF, 1))
    return x * cos + rot * sin_signed


def _top_select(g, n_sel):
    lane = lax.broadcasted_iota(jnp.int32, g.shape, 1).astype(F32)
    sel = jnp.zeros(g.shape, F32)
    for _ in range(n_sel):
        m = jnp.max(g, axis=1, keepdims=True)
        idx = jnp.min(jnp.where(g == m, lane, float(g.shape[1])), axis=1, keepdims=True)
        pick = lane == idx
        sel = jnp.where(pick, jnp.where(m > -jnp.inf, 1.0, sel), sel)
        g = jnp.where(pick, -jnp.inf, g)
    return sel


def _inproj_prompt_kernel(x_ref, wr_ref, wt_ref, fb_ref, cq_ref, sq_ref, ct_ref, st_ref,
                          r_ref, mk_ref, mv_ref, fk_ref, fv_ref, lf_ref):
    x = x_ref[...].astype(BF16)
    r = _mm(x, wr_ref[...])
    t = _mm(wt_ref[...], x, _NT)
    nq = 3 * CONV_W
    r_ref[:, 0:nq] = r[:, 0:nq]
    r_ref[:, nq:nq + MOBA_W] = _rope_rows(r[:, nq:nq + MOBA_W], cq_ref[...], sq_ref[...])
    r_ref[:, nq + MOBA_W:] = r[:, nq + MOBA_W:]
    cos = ct_ref[...]
    sin = st_ref[...]
    for h in range(MOBA_H):
        a = h * HEAD_DIM
        x1 = t[a:a + HALF]
        x2 = t[a + HALF:a + HEAD_DIM]
        mk_ref[a:a + HALF, :] = x1 * cos - x2 * sin
        mk_ref[a + HALF:a + HEAD_DIM, :] = x2 * cos + x1 * sin
    mv_ref[...] = t[MOBA_W:2 * MOBA_W]
    fk_ref[...] = t[2 * MOBA_W:2 * MOBA_W + FOX_W]
    nt = 2 * MOBA_W + 2 * FOX_W
    fv_ref[...] = t[2 * MOBA_W + FOX_W:nt]
    lf_ref[0] = _log_sigmoid(t[nt:nt + SUBLANES] + fb_ref[...])


def _inproj_prompt(x, wr, wt, fb, cosq, sinq, cost, sint, batch, seq, tm):
    n, d = x.shape
    nts = seq // tm
    nr = wr.shape[1]
    ntr = wt.shape[0]
    kv_shape = jax.ShapeDtypeStruct((batch, MOBA_W, seq), F32)
    kv_spec = pl.BlockSpec((None, MOBA_W, tm), lambda i: (i // nts, 0, i % nts))
    return pl.pallas_call(
        _inproj_prompt_kernel,
        grid=(n // tm,),
        in_specs=[
            pl.BlockSpec((tm, d), lambda i: (i, 0)),
            pl.BlockSpec((d, nr), lambda i: (0, 0)),
            pl.BlockSpec((ntr, d), lambda i: (0, 0)),
            pl.BlockSpec((SUBLANES, 1), lambda i: (0, 0)),
            pl.BlockSpec((tm, MOBA_W), lambda i: (i % nts, 0)),
            pl.BlockSpec((tm, MOBA_W), lambda i: (i % nts, 0)),
            pl.BlockSpec((HALF, tm), lambda i: (0, i % nts)),
            pl.BlockSpec((HALF, tm), lambda i: (0, i % nts)),
        ],
        out_specs=[
            pl.BlockSpec((tm, nr), lambda i: (i, 0)),
            kv_spec, kv_spec, kv_spec, kv_spec,
            pl.BlockSpec((1, SUBLANES, tm), lambda i: (i // nts, 0, i % nts)),
        ],
        out_shape=[jax.ShapeDtypeStruct((n, nr), F32), kv_shape, kv_shape, kv_shape, kv_shape,
                   jax.ShapeDtypeStruct((batch, SUBLANES, seq), F32)],
        compiler_params=_cparams(("arbitrary",)),
        name="inproj_prompt",
    )(x, wr, wt, fb, cosq, sinq, cost, sint)


def _inproj_sample_kernel(x_ref, w_ref, fb_ref, cq_ref, sq_ref, z_ref):
    j = pl.program_id(0)
    z = _dot(x_ref[...], w_ref[...], True)
    roped = _rope_rows(z, cq_ref[...], sq_ref[...])
    gate = _log_sigmoid(z + fb_ref[...])
    is_rope = jnp.logical_or(j == C_MQ // MOBA_W, j == C_MK // MOBA_W)
    z_ref[...] = jnp.where(is_rope, roped, jnp.where(j == C_FG // MOBA_W, gate, z))


def _inproj_sample(x, w, fb, cosq, sinq):
    n, d = x.shape
    tn = MOBA_W
    return pl.pallas_call(
        _inproj_sample_kernel,
        grid=(w.shape[1] // tn,),
        in_specs=[
            pl.BlockSpec((n, d), lambda j: (0, 0)),
            pl.BlockSpec((d, tn), lambda j: (0, j)),
            pl.BlockSpec((1, tn), lambda j: (0, 0)),
            pl.BlockSpec((n, tn), lambda j: (0, 0)),
            pl.BlockSpec((n, tn), lambda j: (0, 0)),
        ],
        out_specs=pl.BlockSpec((n, tn), lambda j: (0, j)),
        out_shape=jax.ShapeDtypeStruct((n, w.shape[1]), F32),
        compiler_params=_cparams(("arbitrary",)),
        name="inproj_sample",
    )(x, w, fb, cosq, sinq)


def _attn_prompt_kernel(*refs, fox, nblk):
    if fox:
        q_ref, kt_ref, vt_ref, lf_ref, tri_ref, o_ref, c_ref = refs
    else:
        q_ref, kt_ref, vt_ref, o_ref, km_ref = refs
    p = pl.program_id(1)
    i = pl.program_id(2)
    tq = q_ref.shape[0]
    bs = MOBA_BLOCK
    lane = lax.broadcasted_iota(jnp.int32, (tq, LANES), 1)
    q = q_ref[...] * (HEAD_DIM ** -0.5)
    qh = (jnp.where(lane < HEAD_DIM, q, 0.0), jnp.where(lane < HEAD_DIM, 0.0, q))
    qb = tuple(v.astype(BF16) for v in qh)

    if fox:
        @pl.when(i == 0)
        def _():
            carry = jnp.zeros((SUBLANES, 1), F32)
            for j in range(nblk):
                blk = _dot_exact_rhs(lf_ref[0, :, j * bs:(j + 1) * bs], tri_ref[...]) + carry
                c_ref[:, j * bs:(j + 1) * bs] = blk
                carry = blk[:, bs - 1:bs]
        sel = None
    else:
        @pl.when(i == 0)
        def _():
            km = jnp.zeros((LANES, LANES), F32)
            kl = lax.broadcasted_iota(jnp.int32, (LANES, LANES), 1)
            for j in range(nblk):
                col = jnp.mean(kt_ref[:, j * bs:(j + 1) * bs], axis=1, keepdims=True)
                km = jnp.where(kl == j, col, km)
            km_ref[...] = km
        sel = []
        for h in range(2):
            g = _dot(qh[h], km_ref[...], True)
            g = jnp.where(lane < i, g, -jnp.inf)
            sel.append(_top_select(g, min(MOBA_TOPK, nblk)))

    def bias_rows(h, off):
        row = c_ref[pl.ds(2 * p + h, 1), pl.ds(off, bs)]
        return -row

    q2 = jnp.concatenate(qb, axis=0)

    def step(off, j, state, diag):
        kt = kt_ref[:, pl.ds(off, bs)].astype(BF16)
        vt = vt_ref[:, pl.ds(off, bs)].astype(BF16)
        s2 = _mm(q2, kt)
        stats, ps = [], []
        for h in range(2):
            m, l, _ = state[h]
            s = s2[h * tq:(h + 1) * tq]
            if fox:
                s = s + bias_rows(h, off)
            else:
                if not diag:
                    on = jnp.max(jnp.where(lane == j, sel[h], 0.0), axis=1, keepdims=True)
                    s = jnp.where(on > 0.0, s, -jnp.inf)
            if diag:
                r_i = lax.broadcasted_iota(jnp.int32, (tq, bs), 0)
                c_i = lax.broadcasted_iota(jnp.int32, (tq, bs), 1)
                s = jnp.where(c_i <= r_i, s, -jnp.inf)
            m_new = jnp.maximum(m, jnp.max(s, axis=1, keepdims=True))
            alpha = jnp.exp(m - m_new)
            pexp = jnp.exp(s - m_new)
            stats.append((m_new, l * alpha + jnp.sum(pexp, axis=1, keepdims=True), alpha))
            ps.append(pexp.astype(BF16))
        pv2 = _mm(jnp.concatenate(ps, axis=0), vt, _NT)
        return tuple((stats[h][0], stats[h][1], state[h][2] * stats[h][2] + pv2[h * tq:(h + 1) * tq])
                     for h in range(2))

    init = tuple((jnp.full((tq, 1), NEG_BIG, F32), jnp.zeros((tq, 1), F32), jnp.zeros((tq, LANES), F32))
                 for _ in range(2))
    state = step(pl.multiple_of(i * bs, bs), i, init, True)

    def body(j, st):
        return step(pl.multiple_of(j * bs, bs), j, st, False)

    state = lax.fori_loop(0, i, body, state)
    o0 = state[0][2] / state[0][1]
    o1 = state[1][2] / state[1][1]
    o_ref[...] = jnp.where(lane < HEAD_DIM, o0, o1)


def _attn_prompt(r, k_arr, v_arr, lf, tri, batch, seq, fox):
    n = r.shape[0]
    tq = MOBA_BLOCK
    nq = seq // tq
    npair = (FOX_H if fox else MOBA_H) // 2
    qcol0 = (3 * CONV_W + (MOBA_W if fox else 0)) // LANES
    in_specs = [
        pl.BlockSpec((tq, LANES), lambda b, p, i: (b * nq + i, qcol0 + p)),
        pl.BlockSpec((None, LANES, seq), lambda b, p, i: (b, p, 0)),
        pl.BlockSpec((None, LANES, seq), lambda b, p, i: (b, p, 0)),
    ]
    args = [r, k_arr, v_arr]
    if fox:
        in_specs += [pl.BlockSpec((1, SUBLANES, seq), lambda b, p, i: (b, 0, 0)),
                     pl.BlockSpec((tq, tq), lambda b, p, i: (0, 0))]
        args += [lf, tri]
        scratch = [pltpu.VMEM((SUBLANES, seq), F32)]
    else:
        scratch = [pltpu.VMEM((LANES, LANES), F32)]
    return pl.pallas_call(
        functools.partial(_attn_prompt_kernel, fox=fox, nblk=nq),
        grid=(batch, npair, nq),
        in_specs=in_specs,
        out_specs=pl.BlockSpec((tq, LANES), lambda b, p, i: (b * nq + i, p)),
        out_shape=jax.ShapeDtypeStruct((n, npair * LANES), F32),
        scratch_shapes=scratch,
        compiler_params=_cparams(("arbitrary", "arbitrary", "arbitrary")),
        name="fox_prompt" if fox else "moba_prompt",
    )(*args)


def _block_diag_q(q):
    nh = q.shape[1] // HEAD_DIM
    rows = nh * q.shape[0]
    qt = jnp.concatenate([q] * nh, axis=0)
    r_i = lax.broadcasted_iota(jnp.int32, (rows, q.shape[1]), 0)
    c_i = lax.broadcasted_iota(jnp.int32, (rows, q.shape[1]), 1)
    return jnp.where(r_i // q.shape[0] == c_i // HEAD_DIM, qt, 0.0)


def _diag_heads(o, t):
    nh = o.shape[1] // HEAD_DIM
    c_i = lax.broadcasted_iota(jnp.int32, (t, o.shape[1]), 1)
    out = jnp.zeros((t, o.shape[1]), F32)
    for h in range(nh):
        out = jnp.where(c_i // HEAD_DIM == h, o[h * t:(h + 1) * t], out)
    return out


def _expand_heads(c, t):
    nh = FOX_H
    return jnp.concatenate([jnp.broadcast_to(c[h:h + 1], (t, c.shape[1])) for h in range(nh)], axis=0)


def _softmax_update(state, s, v, dims, hi=False):
    m, l, acc = state
    m_new = jnp.maximum(m, jnp.max(s, axis=1, keepdims=True))
    alpha = jnp.exp(m - m_new)
    pexp = jnp.exp(s - m_new)
    l = l * alpha + jnp.sum(pexp, axis=1, keepdims=True)
    acc = acc * alpha + _dot(pexp, v, hi, dims)
    return m_new, l, acc


def _page_copies(pt_ref, srcs, bufs, sem_ref, layer, seq, chunk, slot, npg):
    copies = []
    for k in range(npg):
        page = pt_ref[seq, chunk * npg + k]
        for j, (src, buf) in enumerate(zip(srcs, bufs)):
            copies.append(pltpu.make_async_copy(src(layer, page), buf.at[slot, k], sem_ref.at[j, slot]))
    return copies


def _fox_decode_kernel(pt_ref, q_ref, kn_ref, vn_ref, lfn_ref, tri_ref, tri8_ref, ck_hbm, cv_hbm, clf_hbm,
                       o_ref, m_ref, l_ref, acc_ref, carry_ref, kcat_ref, vcat_ref, kbuf, vbuf, lfbuf, sem,
                       *, layer, npg, tsz):
    b = pl.program_id(0)
    c = pl.program_id(1)
    nb = pl.num_programs(0)
    nc = pl.num_programs(1)
    rows = FOX_H * tsz
    step = b * nc + c
    slot = step % 2
    srcs = (lambda l, p: ck_hbm.at[l, p], lambda l, p: cv_hbm.at[l, p],
            lambda l, p: clf_hbm.at[l, :, pl.ds(p, 1), :])
    bufs = (kbuf, vbuf, lfbuf)

    @pl.when(step == 0)
    def _():
        for cp in _page_copies(pt_ref, srcs, bufs, sem, layer, b, c, slot, npg):
            cp.start()

    @pl.when(step + 1 < nb * nc)
    def _():
        nxt = step + 1
        for cp in _page_copies(pt_ref, srcs, bufs, sem, layer, nxt // nc, nxt % nc, 1 - slot, npg):
            cp.start()

    @pl.when(c == 0)
    def _():
        m_ref[...] = jnp.full(m_ref.shape, NEG_BIG, F32)
        l_ref[...] = jnp.zeros(l_ref.shape, F32)
        acc_ref[...] = jnp.zeros(acc_ref.shape, F32)
        carry_ref[...] = jnp.zeros(carry_ref.shape, F32)

    qbd = _block_diag_q(q_ref[...] * (HEAD_DIM ** -0.5))
    for cp in _page_copies(pt_ref, srcs, bufs, sem, layer, b, c, slot, npg):
        cp.wait()
    state = (m_ref[:, 0:1], l_ref[:, 0:1], acc_ref[...])
    carry = carry_ref[:, 0:1]
    sub = lax.broadcasted_iota(jnp.int32, (SUBLANES, LANES), 0)
    lfs = []
    for k in range(npg):
        lf = jnp.zeros((SUBLANES, LANES), F32)
        for h in range(FOX_H):
            lf = jnp.where(sub == h, jnp.broadcast_to(lfbuf[slot, k, h], (SUBLANES, LANES)), lf)
        lfs.append(lf)
    cum = _dot_exact_rhs(jnp.concatenate(lfs, axis=0), tri_ref[...])
    cs = []
    for k in range(npg):
        ck = cum[k * SUBLANES:(k + 1) * SUBLANES] + carry
        carry = ck[:, LANES - 1:LANES]
        cs.append(_expand_heads(ck, tsz))
        kcat_ref[:, k * LANES:(k + 1) * LANES] = kbuf[slot, k].astype(BF16)
        vcat_ref[:, k * LANES:(k + 1) * LANES] = vbuf[slot, k].astype(BF16)
    last = (npg - 1) * LANES
    q_hi, q_lo = _split2(qbd)
    k_hi, k_lo = _split2(kbuf[slot, npg - 1])
    v_hi, v_lo = _split2(vbuf[slot, npg - 1])
    s = _mm(q_hi, kcat_ref[...])
    s = jnp.concatenate([s[:, 0:last], s[:, last:] + (_mm(q_hi, k_lo) + _mm(q_lo, k_hi))], axis=1)
    s = s - jnp.concatenate(cs, axis=1)
    m, l, acc = state
    m_new = jnp.maximum(m, jnp.max(s, axis=1, keepdims=True))
    alpha = jnp.exp(m - m_new)
    pexp = jnp.exp(s - m_new)
    p_hi, p_lo = _split2(pexp[:, last:])
    pv = _mm(pexp.astype(BF16), vcat_ref[...], _NT) + (_mm(p_hi, v_lo, _NT) + _mm(p_lo, v_hi, _NT))
    state = (m_new, l * alpha + jnp.sum(pexp, axis=1, keepdims=True), acc * alpha + pv)

    @pl.when(c < nc - 1)
    def _():
        m_ref[...] = jnp.broadcast_to(state[0], m_ref.shape)
        l_ref[...] = jnp.broadcast_to(state[1], l_ref.shape)
        acc_ref[...] = state[2]
        carry_ref[...] = jnp.broadcast_to(carry, carry_ref.shape)

    @pl.when(c == nc - 1)
    def _():
        cn = _dot_exact_rhs(lfn_ref[...], tri8_ref[...]) + carry
        s = _dot(qbd, kn_ref[...], True, _NT) - _expand_heads(cn, tsz)
        r_i = lax.broadcasted_iota(jnp.int32, (rows, tsz), 0)
        c_i = lax.broadcasted_iota(jnp.int32, (rows, tsz), 1)
        s = jnp.where(c_i <= r_i % tsz, s, -jnp.inf)
        _, l, acc = _softmax_update(state, s, vn_ref[...], _NN, True)
        o_ref[...] = _diag_heads(acc / l, tsz)


def _fox_decode(pt, zs, lfn, cache_k, cache_v, cache_lf, tri, tri8, layer, npg):
    nseq, npages = pt.shape
    tsz = zs.shape[0] // nseq
    nc = npages // npg
    rows = FOX_H * tsz
    any_spec = pl.BlockSpec(memory_space=pl.ANY)
    in_specs = [
        pl.BlockSpec((tsz, FOX_W), lambda b, c, pt: (b, C_FQ // FOX_W)),
        pl.BlockSpec((tsz, FOX_W), lambda b, c, pt: (b, C_FK // FOX_W)),
        pl.BlockSpec((tsz, FOX_W), lambda b, c, pt: (b, C_FV // FOX_W)),
        pl.BlockSpec((None, SUBLANES, tsz), lambda b, c, pt: (b, 0, 0)),
        pl.BlockSpec((LANES, LANES), lambda b, c, pt: (0, 0)),
        pl.BlockSpec((tsz, tsz), lambda b, c, pt: (0, 0)),
        any_spec, any_spec, any_spec,
    ]
    grid_spec = pltpu.PrefetchScalarGridSpec(
        num_scalar_prefetch=1,
        grid=(nseq, nc),
        in_specs=in_specs,
        out_specs=pl.BlockSpec((tsz, FOX_W), lambda b, c, pt: (b, 0)),
        scratch_shapes=[pltpu.VMEM((rows, LANES), F32), pltpu.VMEM((rows, LANES), F32),
                        pltpu.VMEM((rows, FOX_W), F32), pltpu.VMEM((SUBLANES, LANES), F32),
                        pltpu.VMEM((FOX_W, npg * LANES), BF16), pltpu.VMEM((FOX_W, npg * LANES), BF16),
                        pltpu.VMEM((2, npg, FOX_W, LANES), F32), pltpu.VMEM((2, npg, FOX_W, LANES), F32),
                        pltpu.VMEM((2, npg, FOX_H, 1, LANES), F32), pltpu.SemaphoreType.DMA((3, 2))],
    )
    return pl.pallas_call(
        functools.partial(_fox_decode_kernel, layer=layer, npg=npg, tsz=tsz),
        grid_spec=grid_spec,
        out_shape=jax.ShapeDtypeStruct((nseq * tsz, FOX_W), F32),
        compiler_params=_cparams(("arbitrary", "arbitrary")),
        name="fox_decode",
    )(pt, zs, zs, zs, lfn, tri, tri8, cache_k, cache_v, cache_lf)


def _moba_decode_kernel(pt_ref, q_ref, kn_ref, vn_ref, ck_hbm, cv_hbm, o_ref, s_ref, g_ref, sel_ref, m_ref, l_ref,
                        acc_ref, cat_ref, pbuf, sem, *, layer, npg, tsz):
    b = pl.program_id(0)
    ph = pl.program_id(1)
    c = pl.program_id(2)
    nb = pl.num_programs(0)
    nc = pl.num_programs(2)
    rows = MOBA_H * tsz
    ppb = MOBA_BLOCK // LANES
    nblk = npg // ppb
    step = (b * 2 + ph) * nc + c
    slot = step % 2
    bufs = (pbuf,)
    k_src = (lambda l, p: ck_hbm.at[l, p],)
    v_src = (lambda l, p: cv_hbm.at[l, p],)

    def copies(srcs, st, sl):
        return _page_copies(pt_ref, srcs, bufs, sem, layer, st // (2 * nc), st % nc, sl, npg)

    @pl.when(step == 0)
    def _():
        for cp in copies(k_src, step, slot):
            cp.start()

    nxt = step + 1
    nxt_ph = (nxt // nc) % 2

    @pl.when(jnp.logical_and(nxt < nb * 2 * nc, nxt_ph == 0))
    def _():
        for cp in copies(k_src, nxt, 1 - slot):
            cp.start()

    @pl.when(jnp.logical_and(nxt < nb * 2 * nc, nxt_ph == 1))
    def _():
        for cp in copies(v_src, nxt, 1 - slot):
            cp.start()

    qbd = _block_diag_q(q_ref[...] * (HEAD_DIM ** -0.5)).astype(BF16)
    lane = lax.broadcasted_iota(jnp.int32, (rows, LANES), 1)
    off = pl.multiple_of(c * (npg * LANES), npg * LANES)

    @pl.when(ph == 0)
    def _():
        for cp in copies(k_src, step, slot):
            cp.wait()

        @pl.when(c == 0)
        def _():
            g_ref[...] = jnp.full(g_ref.shape, -jnp.inf, F32)
        g = g_ref[...]
        for k in range(npg):
            cat_ref[:, k * LANES:(k + 1) * LANES] = pbuf[slot, k].astype(BF16)
        s = _mm(qbd, cat_ref[...])
        s_ref[:, pl.ds(off, npg * LANES)] = s
        for jb in range(nblk):
            tot = jnp.sum(s[:, jb * MOBA_BLOCK:(jb + 1) * MOBA_BLOCK], axis=1, keepdims=True)
            g = jnp.where(lane == c * nblk + jb, tot * (1.0 / MOBA_BLOCK), g)
        g_ref[...] = g

    @pl.when(ph == 1)
    def _():
        for cp in copies(v_src, step, slot):
            cp.wait()

        @pl.when(c == 0)
        def _():
            sel_ref[...] = _top_select(g_ref[...], MOBA_TOPK)
            s = _mm(qbd, kn_ref[...].astype(BF16), _NT)
            r_i = lax.broadcasted_iota(jnp.int32, (rows, tsz), 0)
            c_i = lax.broadcasted_iota(jnp.int32, (rows, tsz), 1)
            s = jnp.where(c_i <= r_i % tsz, s, -jnp.inf)
            init = (jnp.full((rows, 1), NEG_BIG, F32), jnp.zeros((rows, 1), F32),
                    jnp.zeros((rows, MOBA_W), F32))
            m, l, acc = _softmax_update(init, s, vn_ref[...], _NN)
            m_ref[...] = jnp.broadcast_to(m, m_ref.shape)
            l_ref[...] = jnp.broadcast_to(l, l_ref.shape)
            acc_ref[...] = acc

        m, l, acc = m_ref[:, 0:1], l_ref[:, 0:1], acc_ref[...]
        sel = sel_ref[...]
        sc = s_ref[:, pl.ds(off, npg * LANES)]
        ss = []
        for jb in range(nblk):
            on = jnp.max(jnp.where(lane == c * nblk + jb, sel, 0.0), axis=1, keepdims=True)
            ss.append(jnp.where(on > 0.0, sc[:, jb * MOBA_BLOCK:(jb + 1) * MOBA_BLOCK], -jnp.inf))
        for k in range(npg):
            cat_ref[:, k * LANES:(k + 1) * LANES] = pbuf[slot, k].astype(BF16)
        s = jnp.concatenate(ss, axis=1)
        m_new = jnp.maximum(m, jnp.max(s, axis=1, keepdims=True))
        alpha = jnp.exp(m - m_new)
        pexp = jnp.exp(s - m_new)
        pv = _mm(pexp.astype(BF16), cat_ref[...], _NT)
        l_new = l * alpha + jnp.sum(pexp, axis=1, keepdims=True)
        acc_new = acc * alpha + pv
        m_ref[...] = jnp.broadcast_to(m_new, m_ref.shape)
        l_ref[...] = jnp.broadcast_to(l_new, l_ref.shape)
        acc_ref[...] = acc_new

        @pl.when(c == nc - 1)
        def _():
            o_ref[...] = _diag_heads(acc_new / l_new, tsz)


def _moba_decode(pt, zs, cache_k, cache_v, layer, npg):
    nseq, npages = pt.shape
    tsz = zs.shape[0] // nseq
    nc = npages // npg
    rows = MOBA_H * tsz
    any_spec = pl.BlockSpec(memory_space=pl.ANY)
    in_specs = [
        pl.BlockSpec((tsz, MOBA_W), lambda b, ph, c, pt: (b, C_MQ // MOBA_W)),
        pl.BlockSpec((tsz, MOBA_W), lambda b, ph, c, pt: (b, C_MK // MOBA_W)),
        pl.BlockSpec((tsz, MOBA_W), lambda b, ph, c, pt: (b, C_MV // MOBA_W)),
        any_spec, any_spec,
    ]
    grid_spec = pltpu.PrefetchScalarGridSpec(
        num_scalar_prefetch=1,
        grid=(nseq, 2, nc),
        in_specs=in_specs,
        out_specs=pl.BlockSpec((tsz, MOBA_W), lambda b, ph, c, pt: (b, 0)),
        scratch_shapes=[pltpu.VMEM((rows, npages * LANES), F32), pltpu.VMEM((rows, LANES), F32),
                        pltpu.VMEM((rows, LANES), F32), pltpu.VMEM((rows, LANES), F32),
                        pltpu.VMEM((rows, LANES), F32), pltpu.VMEM((rows, MOBA_W), F32),
                        pltpu.VMEM((MOBA_W, npg * LANES), BF16),
                        pltpu.VMEM((2, npg, MOBA_W, LANES), F32), pltpu.SemaphoreType.DMA((1, 2))],
    )
    return pl.pallas_call(
        functools.partial(_moba_decode_kernel, layer=layer, npg=npg, tsz=tsz),
        grid_spec=grid_spec,
        out_shape=jax.ShapeDtypeStruct((nseq * tsz, MOBA_W), F32),
        compiler_params=_cparams(("arbitrary", "arbitrary", "arbitrary")),
        name="moba_decode",
    )(pt, zs, zs, zs, cache_k, cache_v)


def _outproj_kernel(*refs, alpha, sample, hi, nts):
    if sample:
        x_ref, cb_ref, cc_ref, cx_ref, h0_ref, h1_ref, mo_ref, fo_ref, wo_ref, cw_ref, g_ref, b_ref, \
            y_ref, u_ref = refs
    else:
        x_ref, cb_ref, cc_ref, cx_ref, cch_ref, cxh_ref, mo_ref, fo_ref, wo_ref, cw_ref, g_ref, b_ref, \
            y_ref, u_ref = refs
    u = cc_ref[...] * cx_ref[...]
    tm = u.shape[0]
    row = lax.broadcasted_iota(jnp.int32, u.shape, 0)
    r1 = pltpu.roll(u, 1, 0)
    r2 = pltpu.roll(u, 2, 0)
    if sample:
        t = row % SUBLANES
        s1 = jnp.where(t == 0, h1_ref[...], r1)
        s2 = jnp.where(t == 0, h0_ref[...], jnp.where(t == 1, h1_ref[...], r2))
        u_ref[...] = u
    else:
        first = pl.program_id(0) % nts == 0
        uh = jnp.where(first, 0.0, cch_ref[...] * cxh_ref[...])
        p1 = jnp.broadcast_to(uh[SUBLANES - 1:SUBLANES], u.shape)
        p2 = jnp.broadcast_to(uh[SUBLANES - 2:SUBLANES - 1], u.shape)
        s1 = jnp.where(row == 0, p1, r1)
        s2 = jnp.where(row == 0, p2, jnp.where(row == 1, p1, r2))
        u_ref[...] = u[tm - SUBLANES:tm]
    cw = cw_ref[...]
    conv = cb_ref[...] * (s2 * cw[0:1] + s1 * cw[1:2] + u * cw[2:3])
    mixed = (_dot(conv, wo_ref[0:CONV_W, :], hi)
             + _dot(mo_ref[...], wo_ref[CONV_W:CONV_W + MOBA_W, :], hi)
             + _dot(fo_ref[...], wo_ref[CONV_W + MOBA_W:, :], hi))
    y_ref[...] = _layer_norm(alpha * x_ref[...] + mixed, g_ref[...], b_ref[...])


def _outproj(x, r, ccol, mo, fo, wo, cw, g, b, alpha, tm, seq, hist=None):
    n, d = x.shape
    sample = hist is not None
    nts = 1 if sample else seq // tm
    hb = tm // SUBLANES
    col = lambda k: pl.BlockSpec((tm, CONV_W), lambda i: (i, ccol + k))
    if sample:
        extra = [pl.BlockSpec((tm, CONV_W), lambda i: (i, 0)), pl.BlockSpec((tm, CONV_W), lambda i: (i, 0))]
        extra_args = list(hist)
        u_shape, u_spec = (n, CONV_W), pl.BlockSpec((tm, CONV_W), lambda i: (i, 0))
    else:
        halo = lambda k: pl.BlockSpec((SUBLANES, CONV_W), lambda i: (jnp.maximum(i * hb - 1, 0), ccol + k))
        extra = [halo(1), halo(2)]
        extra_args = [r, r]
        u_shape, u_spec = (n // tm * SUBLANES, CONV_W), pl.BlockSpec((SUBLANES, CONV_W), lambda i: (i, 0))
    return pl.pallas_call(
        functools.partial(_outproj_kernel, alpha=alpha, sample=sample, hi=sample, nts=nts),
        grid=(n // tm,),
        in_specs=[pl.BlockSpec((tm, d), lambda i: (i, 0)), col(0), col(1), col(2)] + extra + [
            pl.BlockSpec((tm, MOBA_W), lambda i: (i, 0)),
            pl.BlockSpec((tm, FOX_W), lambda i: (i, 0)),
            pl.BlockSpec(wo.shape, lambda i: (0, 0)),
            pl.BlockSpec(cw.shape, lambda i: (0, 0)),
            pl.BlockSpec((1, d), lambda i: (0, 0)),
            pl.BlockSpec((1, d), lambda i: (0, 0)),
        ],
        out_specs=[pl.BlockSpec((tm, d), lambda i: (i, 0)), u_spec],
        out_shape=[jax.ShapeDtypeStruct((n, d), F32), jax.ShapeDtypeStruct(u_shape, F32)],
        compiler_params=_cparams(("arbitrary",)),
        name="outproj_sample" if sample else "outproj_prompt",
    )(x, r, r, r, *extra_args, mo, fo, wo, cw, g, b)


def _ffn_kernel(src_ref, eid_ref, ns_ref, *refs, g, hi):
    x_refs = refs[:g]
    wg_ref, wu_ref, wd_ref, o_ref, acc_ref = refs[g:]
    i = pl.program_id(0)
    f = pl.program_id(1)
    last = f == pl.num_programs(1) - 1
    live = i < ns_ref[0]

    @pl.when(live)
    def _():
        x = x_refs[0][...] if g == 1 else jnp.concatenate([r[...] for r in x_refs], axis=0)
        h = _silu(_dot(x, wg_ref[...], hi)) * _dot(x, wu_ref[...], hi)
        part = _dot(h, wd_ref[...], hi)

        @pl.when(f == 0)
        def _():
            acc_ref[...] = part

        @pl.when(f > 0)
        def _():
            acc_ref[...] = acc_ref[...] + part

        @pl.when(last)
        def _():
            o_ref[...] = acc_ref[...].astype(o_ref.dtype)

    @pl.when(jnp.logical_and(jnp.logical_not(live), last))
    def _():
        o_ref[...] = jnp.zeros(o_ref.shape, o_ref.dtype)


def _ffn(x, src, eid, nsteps, wg, wu, wd, rows, g, tf, hi, out_dtype):
    d = x.shape[1]
    ff = wg.shape[2]
    ns = eid.shape[0]

    def x_spec(j):
        return pl.BlockSpec((rows, d), lambda i, f, src, eid, n: (src[i * g + j], 0))

    grid_spec = pltpu.PrefetchScalarGridSpec(
        num_scalar_prefetch=3,
        grid=(ns, ff // tf),
        in_specs=[x_spec(j) for j in range(g)] + [
            pl.BlockSpec((None, d, tf), lambda i, f, src, eid, n: (eid[i], 0, f)),
            pl.BlockSpec((None, d, tf), lambda i, f, src, eid, n: (eid[i], 0, f)),
            pl.BlockSpec((None, tf, d), lambda i, f, src, eid, n: (eid[i], f, 0)),
        ],
        out_specs=pl.BlockSpec((g * rows, d), lambda i, f, src, eid, n: (i, 0)),
        scratch_shapes=[pltpu.VMEM((g * rows, d), F32)],
    )
    return pl.pallas_call(
        functools.partial(_ffn_kernel, g=g, hi=hi),
        grid_spec=grid_spec,
        out_shape=jax.ShapeDtypeStruct((ns * g * rows, d), out_dtype),
        compiler_params=_cparams(("arbitrary", "arbitrary")),
        name="ffn_hi" if hi else "ffn",
    )(src, eid, nsteps, *([x] * g), wg, wu, wd)


def _router_kernel(x_ref, w_ref, cwt_ref, cnt_ref, *, n_exp):
    logits = _dot(x_ref[...], w_ref[...], True)
    lane = lax.broadcasted_iota(jnp.int32, logits.shape, 1)
    logits = jnp.where(lane < n_exp, logits, -jnp.inf)
    m = jnp.max(logits, axis=1, keepdims=True)
    e = jnp.exp(logits - m)
    probs = e / jnp.sum(e, axis=1, keepdims=True)
    sel = _top_select(jnp.where(lane < n_exp, probs, -jnp.inf), TOP_K)
    top = probs * sel
    cw = top / jnp.sum(top, axis=1, keepdims=True)
    cwt_ref[...] = jnp.transpose(cw)[0:SUBLANES]
    cnt_ref[...] = jnp.broadcast_to(jnp.sum(sel, axis=0, keepdims=True), cnt_ref.shape)


def _router(x, w, n_exp, tm):
    n, d = x.shape
    return pl.pallas_call(
        functools.partial(_router_kernel, n_exp=n_exp),
        grid=(n // tm,),
        in_specs=[pl.BlockSpec((tm, d), lambda i: (i, 0)), pl.BlockSpec((d, LANES), lambda i: (0, 0))],
        out_specs=[pl.BlockSpec((SUBLANES, tm), lambda i: (0, i)),
                   pl.BlockSpec((SUBLANES, LANES), lambda i: (i, 0))],
        out_shape=[jax.ShapeDtypeStruct((SUBLANES, n), F32),
                   jax.ShapeDtypeStruct((n // tm * SUBLANES, LANES), F32)],
        compiler_params=_cparams(("arbitrary",)),
        name="router",
    )(x, w)


def _slot_rows(cwt_ref, tri_ref):
    on = cwt_ref[...] > 0.0
    cum = _mm(jnp.where(on, 1.0, 0.0).astype(BF16), tri_ref[...])
    return jnp.where(on, cum - 1.0, -1.0)


def _pack_kernel(cnt_ref, x_ref, cwt_ref, tri_ref, xc_ref, xb_ref, pos_ref):
    t = pl.program_id(0)
    e = pl.program_id(1)
    tm = x_ref.shape[0]

    @pl.when(e == 0)
    def _():
        xb_ref[...] = x_ref[...].astype(BF16)
        pos_ref[...] = _slot_rows(cwt_ref, tri_ref)

    row = pos_ref[pl.ds(e, 1), :]
    sub = lax.broadcasted_iota(jnp.int32, (ROUTE_CHUNK, tm), 0).astype(F32)
    for k in range(tm // ROUTE_CHUNK):
        lo = k * ROUTE_CHUNK

        @pl.when(cnt_ref[t, e] > lo)
        def _():
            sel = jnp.where(row - float(lo) == sub, 1.0, 0.0).astype(BF16)
            xc_ref[lo:lo + ROUTE_CHUNK, :] = _mm(sel, xb_ref[...]).astype(BF16)

        @pl.when(cnt_ref[t, e] <= lo)
        def _():
            xc_ref[lo:lo + ROUTE_CHUNK, :] = jnp.zeros((ROUTE_CHUNK, xc_ref.shape[1]), BF16)


def _pack(x, cwt, counts, tri, tm):
    n, d = x.shape
    n_exp = counts.shape[1]
    grid_spec = pltpu.PrefetchScalarGridSpec(
        num_scalar_prefetch=1,
        grid=(n // tm, n_exp),
        in_specs=[
            pl.BlockSpec((tm, d), lambda t, e, c: (t, 0)),
            pl.BlockSpec((SUBLANES, tm), lambda t, e, c: (0, t)),
            pl.BlockSpec((tm, tm), lambda t, e, c: (0, 0)),
        ],
        out_specs=pl.BlockSpec((tm, d), lambda t, e, c: (t * n_exp + e, 0)),
        scratch_shapes=[pltpu.VMEM((tm, d), BF16), pltpu.VMEM((SUBLANES, tm), F32)],
    )
    return pl.pallas_call(
        _pack_kernel,
        grid_spec=grid_spec,
        out_shape=jax.ShapeDtypeStruct((n * n_exp, d), BF16),
        compiler_params=_cparams(("arbitrary", "arbitrary")),
        name="moe_pack",
    )(counts, x, cwt, tri)


def _unpack_kernel(spos_ref, se_ref, sk_ref, x_ref, cwt_ref, tri_ref, g_ref, b_ref, *rest, alpha, nslot):
    y_refs = rest[:nslot]
    o_ref, pos_ref = rest[nslot:]
    t = pl.program_id(0)
    tm = x_ref.shape[0]
    pos_ref[...] = _slot_rows(cwt_ref, tri_ref)
    sub = lax.broadcasted_iota(jnp.int32, (ROUTE_CHUNK, tm), 0).astype(F32)
    acc = alpha * x_ref[...]
    for s in range(nslot):
        e = se_ref[t, s]
        ee = jnp.maximum(e, 0)
        lo = (sk_ref[t, s] * ROUTE_CHUNK).astype(F32)
        row = pos_ref[pl.ds(ee, 1), :]
        w = jnp.where(e >= 0, cwt_ref[pl.ds(ee, 1), :], 0.0)
        sel = jnp.where(row - lo == sub, w, 0.0)
        acc = acc + _mm(jnp.transpose(sel).astype(BF16), y_refs[s][...])
    o_ref[...] = _layer_norm(acc, g_ref[...], b_ref[...])


def _unpack(x, y, cwt, tri, slot_pos, slot_e, slot_k, g, b, alpha, tm):
    n, d = x.shape
    nslot = slot_pos.shape[1]

    def y_spec(s):
        return pl.BlockSpec((ROUTE_CHUNK, d), lambda t, sp, se, sk: (sp[t, s], 0))

    grid_spec = pltpu.PrefetchScalarGridSpec(
        num_scalar_prefetch=3,
        grid=(n // tm,),
        in_specs=[
            pl.BlockSpec((tm, d), lambda t, sp, se, sk: (t, 0)),
            pl.BlockSpec((SUBLANES, tm), lambda t, sp, se, sk: (0, t)),
            pl.BlockSpec((tm, tm), lambda t, sp, se, sk: (0, 0)),
            pl.BlockSpec((1, d), lambda t, sp, se, sk: (0, 0)),
            pl.BlockSpec((1, d), lambda t, sp, se, sk: (0, 0)),
        ] + [y_spec(s) for s in range(nslot)],
        out_specs=pl.BlockSpec((tm, d), lambda t, sp, se, sk: (t, 0)),
        scratch_shapes=[pltpu.VMEM((SUBLANES, tm), F32)],
    )
    return pl.pallas_call(
        functools.partial(_unpack_kernel, alpha=alpha, nslot=nslot),
        grid_spec=grid_spec,
        out_shape=jax.ShapeDtypeStruct((n, d), F32),
        compiler_params=_cparams(("arbitrary",)),
        name="moe_unpack",
    )(slot_pos, slot_e, slot_k, x, cwt, tri, g, b, *([y] * nslot))


def _residual_ln_kernel(x_ref, y_ref, g_ref, b_ref, o_ref, *, alpha):
    o_ref[...] = _layer_norm(alpha * x_ref[...] + y_ref[...], g_ref[...], b_ref[...])


def _residual_ln(x, y, g, b, alpha, tm):
    n, d = x.shape
    row = pl.BlockSpec((tm, d), lambda i: (i, 0))
    vec = pl.BlockSpec((1, d), lambda i: (0, 0))
    return pl.pallas_call(
        functools.partial(_residual_ln_kernel, alpha=alpha),
        grid=(n // tm,),
        in_specs=[row, row, vec, vec],
        out_specs=row,
        out_shape=jax.ShapeDtypeStruct((n, d), F32),
        compiler_params=_cparams(("arbitrary",)),
        name="residual_ln",
    )(x, y, g, b)


def _rope_tables(pos):
    inv = jnp.power(ROPE_THETA, -jnp.arange(0, HEAD_DIM, 2, dtype=F32) / HEAD_DIM)
    ang = pos.astype(F32)[:, None] * inv[None, :]
    return jnp.cos(ang), jnp.sin(ang)


def _row_tables(cos, sin, nh):
    cq = jnp.tile(jnp.concatenate([cos, cos], axis=1), (1, nh))
    sq = jnp.tile(jnp.concatenate([-sin, sin], axis=1), (1, nh))
    return cq, sq


def _upper_ones(n):
    return (jnp.arange(n)[:, None] <= jnp.arange(n)[None, :]).astype(BF16)


def _route_plan(counts, tm, group):
    nt, n_exp = counts.shape
    kmax = tm // ROUTE_CHUNK
    nslot = n_exp + TOP_K * kmax
    nch = (counts + ROUTE_CHUNK - 1) // ROUTE_CHUNK
    tot = jnp.sum(nch, axis=0)
    tot_pad = (tot + group - 1) // group * group
    off = jnp.cumsum(tot_pad) - tot_pad
    before = jnp.cumsum(nch, axis=0) - nch
    k = jnp.arange(kmax, dtype=jnp.int32)
    lpos = off[None, :, None] + before[:, :, None] + k[None, None, :]
    valid = k[None, None, :] < nch[:, :, None]
    t_i = jnp.arange(nt, dtype=jnp.int32)[:, None, None]
    e_i = jnp.arange(n_exp, dtype=jnp.int32)[None, :, None]
    src = (t_i * n_exp + e_i) * kmax + k[None, None, :]
    lmax = (nt * nslot + n_exp * (group - 1) + group - 1) // group * group
    src_list = jnp.zeros((lmax,), jnp.int32).at[jnp.where(valid, lpos, lmax).reshape(-1)].set(
        src.reshape(-1).astype(jnp.int32), mode="drop")
    step0 = jnp.arange(lmax // group, dtype=jnp.int32) * group
    ends = jnp.cumsum(tot_pad)
    eid = jnp.minimum(jnp.sum(step0[:, None] >= ends[None, :], axis=1), n_exp - 1).astype(jnp.int32)
    nsteps = (jnp.sum(tot_pad) // group).astype(jnp.int32).reshape(1)
    sidx = (jnp.cumsum(nch, axis=1) - nch)[:, :, None] + k[None, None, :]
    sidx = jnp.where(valid, sidx, nslot)
    rows = jnp.broadcast_to(t_i, sidx.shape).reshape(-1)

    def scatter(vals, fill):
        base = jnp.full((nt, nslot), fill, jnp.int32)
        return base.at[rows, sidx.reshape(-1)].set(jnp.broadcast_to(vals, sidx.shape).reshape(-1).astype(jnp.int32),
                                                   mode="drop")
    return src_list, eid, nsteps, scatter(lpos, 0), scatter(e_i, -1), scatter(k[None, None, :], 0)


def _channel_mixer(x, l, alpha, g, b, dense, moe, tm, hi):
    n = x.shape[0]
    nt = n // tm
    i = l // 2
    if l % 2 == 0:
        wg, wu, wd = (w[i:i + 1] for w in dense)
        tm = min(ROUTE_TILE, n)
        nt = n // tm
        y = _ffn(x, jnp.arange(nt, dtype=jnp.int32), jnp.zeros((nt,), jnp.int32), jnp.full((1,), nt, jnp.int32),
                 wg, wu, wd, tm, 1, 512, hi, F32)
        return _residual_ln(x, y, g, b, alpha, tm)
    router_w, wg, wu, wd = moe
    n_exp = wg.shape[1]
    tm = min(ROUTE_TILE, n)
    nt = n // tm
    rw = jnp.pad(router_w[i], ((0, 0), (0, LANES - n_exp)))
    cwt, cnt = _router(x, rw, n_exp, tm)
    counts = cnt.reshape(nt, SUBLANES, LANES)[:, 0, 0:n_exp].astype(jnp.int32)
    tri = _upper_ones(tm)
    xc = _pack(x, cwt, counts, tri, tm)
    src, eid, nsteps, slot_pos, slot_e, slot_k = _route_plan(counts, tm, ROUTE_GROUP)
    y = _ffn(xc, src, eid, nsteps, wg[i], wu[i], wd[i], ROUTE_CHUNK, ROUTE_GROUP, 512, False, BF16)
    return _unpack(x, y, cwt, tri, slot_pos, slot_e, slot_k, g, b, alpha, tm)


def kernel(x_prompt, x_sample, cache_moba_k, cache_moba_v, cache_fox_k, cache_fox_v, cache_fox_logf, state_conv,
           page_table, w_in, conv_w, fox_fbias, w_out, ln_mix_g, ln_mix_b, ln_ffn_g, ln_ffn_b, dense_w_gate,
           dense_w_up, dense_w_down, router_w, moe_w_gate, moe_w_up, moe_w_down):
    depth = w_in.shape[0]
    bp, seq, d = x_prompt.shape
    bs, tsz, _ = x_sample.shape
    n_pages = page_table.shape[1]
    page = cache_moba_k.shape[2]
    past = n_pages * page
    alpha = (2.0 * depth) ** 0.25
    assert page == LANES and tsz == SUBLANES and MOBA_BLOCK % page == 0 and past % MOBA_BLOCK == 0
    assert seq % MOBA_BLOCK == 0

    tm_p = min(512, seq)
    npg = 32
    hp = x_prompt.reshape(bp * seq, d)
    hs = x_sample.reshape(bs * tsz, d)

    cos_p, sin_p = _rope_tables(jnp.arange(seq))
    cq_p, sq_p = _row_tables(cos_p, sin_p, MOBA_H)
    ct_p, st_p = cos_p.T, sin_p.T
    cos_s, sin_s = _rope_tables(past + jnp.arange(tsz))
    cq_s, sq_s = _row_tables(jnp.tile(cos_s, (bs, 1)), jnp.tile(sin_s, (bs, 1)), MOBA_H)

    tri_blk, tri_pg, tri_t = _upper_ones(MOBA_BLOCK), _upper_ones(LANES), _upper_ones(tsz)

    def pages_t(c, w):
        return jnp.transpose(c, (0, 1, 3, 4, 2)).reshape(c.shape[0], c.shape[1], w, page)
    cmk, cmv = pages_t(cache_moba_k, MOBA_W), pages_t(cache_moba_v, MOBA_W)
    cfk, cfv = pages_t(cache_fox_k, FOX_W), pages_t(cache_fox_v, FOX_W)
    clf = jnp.transpose(cache_fox_logf, (0, 3, 1, 2))

    dense_b = tuple(w.astype(BF16) for w in (dense_w_gate, dense_w_up, dense_w_down))
    dense_f = (dense_w_gate, dense_w_up, dense_w_down)
    moe_b = (router_w,) + tuple(w.astype(BF16) for w in (moe_w_gate, moe_w_up, moe_w_down))

    rows_p, rows_s = [], []
    for l in range(depth):
        wl = w_in[l]
        fbias = jnp.pad(fox_fbias[l], (0, SUBLANES - FOX_H))
        wo = w_out[l]
        cwl = jnp.pad(conv_w[l], ((0, SUBLANES - CONV_K), (0, 0)))
        g1, b1 = ln_mix_g[l][None], ln_mix_b[l][None]
        g2, b2 = ln_ffn_g[l][None], ln_ffn_b[l][None]

        w_row = jnp.concatenate([wl[:, 0:C_MK], wl[:, C_FQ:C_FK]], axis=1).astype(BF16)
        w_col = jnp.concatenate([wl[:, C_MK:C_FQ], wl[:, C_FK:C_FG],
                                 jnp.pad(wl[:, C_FG:], ((0, 0), (0, SUBLANES - FOX_H)))], axis=1).T.astype(BF16)
        r, *kv, lf = _inproj_prompt(hp, w_row, w_col, fbias[:, None], cq_p, sq_p, ct_p, st_p, bp, seq, tm_p)
        mo = _attn_prompt(r, kv[0], kv[1], None, None, bp, seq, False)
        fo = _attn_prompt(r, kv[2], kv[3], lf, tri_blk, bp, seq, True)
        hp1, utail = _outproj(hp, r, 0, mo, fo, wo.astype(BF16), cwl, g1, b1, alpha, tm_p, seq)
        hp = _channel_mixer(hp1, l, alpha, g2, b2, dense_b, moe_b, tm_p, False)
        nts = seq // tm_p
        conv_new = utail.reshape(bp, nts, SUBLANES, CONV_W)[:, nts - 1, SUBLANES - (CONV_K - 1):]
        heads_p = [jnp.transpose(a.reshape(bp, MOBA_H, HEAD_DIM, seq), (0, 3, 1, 2)) for a in kv]
        rows_p.append((*heads_p, jnp.transpose(lf[:, 0:FOX_H], (0, 2, 1)), conv_new))

        w_s = jnp.pad(wl, ((0, 0), (0, C_FG + MOBA_W - wl.shape[1])))
        fb_row = jnp.pad(fox_fbias[l], (0, MOBA_W - FOX_H))[None]
        zs = _inproj_sample(hs, w_s, fb_row, cq_s, sq_s)
        lfs = zs[:, C_FG:C_FG + LANES]
        lfn = jnp.transpose(lfs[:, 0:SUBLANES].reshape(bs, tsz, SUBLANES), (0, 2, 1))
        mo_s = _moba_decode(page_table, zs, cmk, cmv, l, npg)
        fo_s = _fox_decode(page_table, zs, lfn, cfk, cfv, clf, tri_pg, tri_t, l, npg)
        st = state_conv[l]
        hist = (jnp.repeat(st[:, 0], tsz, axis=0), jnp.repeat(st[:, 1], tsz, axis=0))
        hs1, us = _outproj(hs, zs, 0, mo_s, fo_s, wo, cwl, g1, b1, alpha, bs * tsz, tsz, hist)
        hs = _channel_mixer(hs1, l, alpha, g2, b2, dense_f, moe_b, bs * tsz, True)
        u_ext = jnp.concatenate([st, us.reshape(bs, tsz, CONV_W)], axis=1)
        heads = lambda c0: zs[:, c0:c0 + MOBA_W].reshape(bs, tsz, MOBA_H, HEAD_DIM)
        rows_s.append((heads(C_MK), heads(C_MV), heads(C_FK), heads(C_FV),
                       lfs[:, 0:FOX_H].reshape(bs, tsz, FOX_H), u_ext[:, tsz:]))

    stack = lambda rows, k: jnp.stack([r[k] for r in rows])
    return (hp.reshape(bp, seq, d), hs.reshape(bs, tsz, d),
            *(stack(rows_p, k) for k in range(6)), *(stack(rows_s, k) for k in range(6)))
```

```python
import functools

import jax
import jax.numpy as jnp
from jax import lax
from jax.experimental import pallas as pl
from jax.experimental.pallas import tpu as pltpu

F32 = jnp.float32
BF16 = jnp.bfloat16

HEAD_DIM = 64
HALF = HEAD_DIM // 2
CONV_W = 256
CONV_K = 3
MOBA_H = 6
FOX_H = 6
MOBA_W = MOBA_H * HEAD_DIM
FOX_W = FOX_H * HEAD_DIM
MOBA_BLOCK = 256
MOBA_TOPK = 3
ROPE_THETA = 10000.0
LN_EPS = 1e-5
TOP_K = 2
LANES = 128
SUBLANES = 8
NEG_BIG = -1e30
VMEM_LIMIT = 56 * 1024 * 1024
ROUTE_CHUNK = 128
ROUTE_GROUP = 8
ROUTE_TILE = 1024
FFN_TILE = 896

C_CB, C_CC, C_CX = 0, CONV_W, 2 * CONV_W
C_MQ = 3 * CONV_W
C_MK = C_MQ + MOBA_W
C_MV = C_MK + MOBA_W
C_FQ = C_MV + MOBA_W
C_FK = C_FQ + FOX_W
C_FV = C_FK + FOX_W
C_FG = C_FV + FOX_W


def _cparams(sem):
    return pltpu.CompilerParams(dimension_semantics=sem, vmem_limit_bytes=VMEM_LIMIT)


def _split2(a):
    hi = a.astype(BF16)
    lo = (a - hi.astype(F32)).astype(BF16)
    return hi, lo


def _split3(a):
    p0 = a.astype(BF16)
    r = a - p0.astype(F32)
    p1 = r.astype(BF16)
    p2 = (r - p1.astype(F32)).astype(BF16)
    return p0, p1, p2


_NN = (((1,), (0,)), ((), ()))
_NT = (((1,), (1,)), ((), ()))


def _mm(a, b, dims=_NN):
    return lax.dot_general(a, b, dims, preferred_element_type=F32)


def _dot(a, b, hi, dims=_NN):
    if not hi:
        return _mm(a.astype(BF16), b.astype(BF16), dims)
    a0, a1 = _split2(a.astype(F32))
    b0, b1 = _split2(b.astype(F32))
    return _mm(a0, b0, dims) + (_mm(a0, b1, dims) + _mm(a1, b0, dims))


def _dot_exact_rhs(a, b_exact, dims=_NN):
    p0, p1, p2 = _split3(a)
    return _mm(p0, b_exact, dims) + (_mm(p1, b_exact, dims) + _mm(p2, b_exact, dims))


def _log_sigmoid(x):
    return jnp.minimum(x, 0.0) - jnp.log1p(jnp.exp(-jnp.abs(x)))


def _silu(x):
    return x * (1.0 / (1.0 + jnp.exp(-x)))


def _layer_norm(y, g, b):
    mu = jnp.mean(y, axis=-1, keepdims=True)
    d = y - mu
    var = jnp.mean(d * d, axis=-1, keepdims=True)
    return d * lax.rsqrt(var + LN_EPS) * g + b


def _rope_rows(x, cos, sin_signed):
    n = x.shape[-1]
    lane = lax.broadcasted_iota(jnp.int32, x.shape, 1)
    first = (lane & (HEAD_DIM - 1)) < HALF
    rot = jnp.where(first, pltpu.roll(x, n - HALF, 1), pltpu.roll(x, HALF, 1))
    return x * cos + rot * sin_signed


def _top_select(g, n_sel):
    lane = lax.broadcasted_iota(jnp.int32, g.shape, 1).astype(F32)
    sel = jnp.zeros(g.shape, F32)
    for _ in range(n_sel):
        m = jnp.max(g, axis=1, keepdims=True)
        idx = jnp.min(jnp.where(g == m, lane, float(g.shape[1])), axis=1, keepdims=True)
        pick = lane == idx
        sel = jnp.where(pick, jnp.where(m > -jnp.inf, 1.0, sel), sel)
        g = jnp.where(pick, -jnp.inf, g)
    return sel


def _inproj_prompt_kernel(x_ref, wr_ref, wt_ref, fb_ref, cq_ref, sq_ref, ct_ref, st_ref,
                          r_ref, mk_ref, mv_ref, fk_ref, fv_ref, lf_ref):
    x = x_ref[...].astype(BF16)
    r = _mm(x, wr_ref[...])
    t = _mm(wt_ref[...], x, _NT)
    nq = 3 * CONV_W
    r_ref[:, 0:nq] = r[:, 0:nq]
    r_ref[:, nq:nq + MOBA_W] = _rope_rows(r[:, nq:nq + MOBA_W], cq_ref[...], sq_ref[...])
    r_ref[:, nq + MOBA_W:] = r[:, nq + MOBA_W:]
    cos = ct_ref[...]
    sin = st_ref[...]
    for h in range(MOBA_H):
        a = h * HEAD_DIM
        x1 = t[a:a + HALF]
        x2 = t[a + HALF:a + HEAD_DIM]
        mk_ref[a:a + HALF, :] = x1 * cos - x2 * sin
        mk_ref[a + HALF:a + HEAD_DIM, :] = x2 * cos + x1 * sin
    mv_ref[...] = t[MOBA_W:2 * MOBA_W]
    fk_ref[...] = t[2 * MOBA_W:2 * MOBA_W + FOX_W]
    nt = 2 * MOBA_W + 2 * FOX_W
    fv_ref[...] = t[2 * MOBA_W + FOX_W:nt]
    lf_ref[0] = _log_sigmoid(t[nt:nt + SUBLANES] + fb_ref[...])


def _inproj_prompt(x, wr, wt, fb, cosq, sinq, cost, sint, batch, seq, tm):
    n, d = x.shape
    nts = seq // tm
    nr = wr.shape[1]
    ntr = wt.shape[0]
    kv_shape = jax.ShapeDtypeStruct((batch, MOBA_W, seq), F32)
    kv_spec = pl.BlockSpec((None, MOBA_W, tm), lambda i: (i // nts, 0, i % nts))
    return pl.pallas_call(
        _inproj_prompt_kernel,
        grid=(n // tm,),
        in_specs=[
            pl.BlockSpec((tm, d), lambda i: (i, 0)),
            pl.BlockSpec((d, nr), lambda i: (0, 0)),
            pl.BlockSpec((ntr, d), lambda i: (0, 0)),
            pl.BlockSpec((SUBLANES, 1), lambda i: (0, 0)),
            pl.BlockSpec((tm, MOBA_W), lambda i: (i % nts, 0)),
            pl.BlockSpec((tm, MOBA_W), lambda i: (i % nts, 0)),
            pl.BlockSpec((HALF, tm), lambda i: (0, i % nts)),
            pl.BlockSpec((HALF, tm), lambda i: (0, i % nts)),
        ],
        out_specs=[
            pl.BlockSpec((tm, nr), lambda i: (i, 0)),
            kv_spec, kv_spec, kv_spec, kv_spec,
            pl.BlockSpec((1, SUBLANES, tm), lambda i: (i // nts, 0, i % nts)),
        ],
        out_shape=[jax.ShapeDtypeStruct((n, nr), F32), kv_shape, kv_shape, kv_shape, kv_shape,
                   jax.ShapeDtypeStruct((batch, SUBLANES, seq), F32)],
        compiler_params=_cparams(("arbitrary",)),
        name="inproj_prompt",
    )(x, wr, wt, fb, cosq, sinq, cost, sint)


def _inproj_sample_kernel(x_ref, w_ref, fb_ref, cq_ref, sq_ref, z_ref):
    j = pl.program_id(0)
    z = _dot(x_ref[...], w_ref[...], True)
    roped = _rope_rows(z, cq_ref[...], sq_ref[...])
    gate = _log_sigmoid(z + fb_ref[...])
    is_rope = jnp.logical_or(j == C_MQ // MOBA_W, j == C_MK // MOBA_W)
    z_ref[...] = jnp.where(is_rope, roped, jnp.where(j == C_FG // MOBA_W, gate, z))


def _inproj_sample(x, w, fb, cosq, sinq):
    n, d = x.shape
    tn = MOBA_W
    return pl.pallas_call(
        _inproj_sample_kernel,
        grid=(w.shape[1] // tn,),
        in_specs=[
            pl.BlockSpec((n, d), lambda j: (0, 0)),
            pl.BlockSpec((d, tn), lambda j: (0, j)),
            pl.BlockSpec((1, tn), lambda j: (0, 0)),
            pl.BlockSpec((n, tn), lambda j: (0, 0)),
            pl.BlockSpec((n, tn), lambda j: (0, 0)),
        ],
        out_specs=pl.BlockSpec((n, tn), lambda j: (0, j)),
        out_shape=jax.ShapeDtypeStruct((n, w.shape[1]), F32),
        compiler_params=_cparams(("arbitrary",)),
        name="inproj_sample",
    )(x, w, fb, cosq, sinq)


def _attn_prompt_kernel(*refs, fox, nblk):
    if fox:
        q_ref, kt_ref, vt_ref, lf_ref, tri_ref, o_ref, c_ref = refs
    else:
        q_ref, kt_ref, vt_ref, o_ref, km_ref = refs
    p = pl.program_id(1)
    i = pl.program_id(2)
    tq = q_ref.shape[0]
    bs = MOBA_BLOCK
    lane = lax.broadcasted_iota(jnp.int32, (tq, LANES), 1)
    q = q_ref[...] * (HEAD_DIM ** -0.5)
    qh = (jnp.where(lane < HEAD_DIM, q, 0.0), jnp.where(lane < HEAD_DIM, 0.0, q))
    qb = tuple(v.astype(BF16) for v in qh)

    if fox:
        @pl.when(i == 0)
        def _():
            carry = jnp.zeros((SUBLANES, 1), F32)
            for j in range(nblk):
                blk = _dot_exact_rhs(lf_ref[0, :, j * bs:(j + 1) * bs], tri_ref[...]) + carry
                c_ref[:, j * bs:(j + 1) * bs] = blk
                carry = blk[:, bs - 1:bs]
        sel = None
    else:
        @pl.when(i == 0)
        def _():
            km = jnp.zeros((LANES, LANES), F32)
            kl = lax.broadcasted_iota(jnp.int32, (LANES, LANES), 1)
            for j in range(nblk):
                col = jnp.mean(kt_ref[:, j * bs:(j + 1) * bs], axis=1, keepdims=True)
                km = jnp.where(kl == j, col, km)
            km_ref[...] = km
        sel = []
        for h in range(2):
            g = _dot(qh[h], km_ref[...], True)
            g = jnp.where(lane < i, g, -jnp.inf)
            sel.append(_top_select(g, min(MOBA_TOPK, nblk)))

    def bias_rows(h, off):
        row = c_ref[pl.ds(2 * p + h, 1), pl.ds(off, bs)]
        return -row

    q2 = jnp.concatenate(qb, axis=0)

    def step(off, j, state, diag):
        kt = kt_ref[:, pl.ds(off, bs)].astype(BF16)
        vt = vt_ref[:, pl.ds(off, bs)].astype(BF16)
        s2 = _mm(q2, kt)
        stats, ps = [], []
        for h in range(2):
            m, l, _ = state[h]
            s = s2[h * tq:(h + 1) * tq]
            if fox:
                s = s + bias_rows(h, off)
            else:
                if not diag:
                    on = jnp.max(jnp.where(lane == j, sel[h], 0.0), axis=1, keepdims=True)
                    s = jnp.where(on > 0.0, s, -jnp.inf)
            if diag:
                r_i = lax.broadcasted_iota(jnp.int32, (tq, bs), 0)
                c_i = lax.broadcasted_iota(jnp.int32, (tq, bs), 1)
                s = jnp.where(c_i <= r_i, s, -jnp.inf)
            m_new = jnp.maximum(m, jnp.max(s, axis=1, keepdims=True))
            alpha = jnp.exp(m - m_new)
            pexp = jnp.exp(s - m_new)
            stats.append((m_new, l * alpha + jnp.sum(pexp, axis=1, keepdims=True), alpha))
            ps.append(pexp.astype(BF16))
        pv2 = _mm(jnp.concatenate(ps, axis=0), vt, _NT)
        return tuple((stats[h][0], stats[h][1], state[h][2] * stats[h][2] + pv2[h * tq:(h + 1) * tq])
                     for h in range(2))

    init = tuple((jnp.full((tq, 1), NEG_BIG, F32), jnp.zeros((tq, 1), F32), jnp.zeros((tq, LANES), F32))
                 for _ in range(2))
    state = step(pl.multiple_of(i * bs, bs), i, init, True)

    def body(j, st):
        return step(pl.multiple_of(j * bs, bs), j, st, False)

    state = lax.fori_loop(0, i, body, state)
    o0 = state[0][2] / state[0][1]
    o1 = state[1][2] / state[1][1]
    o_ref[...] = jnp.where(lane < HEAD_DIM, o0, o1)


def _attn_prompt(r, k_arr, v_arr, lf, tri, batch, seq, fox):
    n = r.shape[0]
    tq = MOBA_BLOCK
    nq = seq // tq
    npair = (FOX_H if fox else MOBA_H) // 2
    qcol0 = (3 * CONV_W + (MOBA_W if fox else 0)) // LANES
    in_specs = [
        pl.BlockSpec((tq, LANES), lambda b, p, i: (b * nq + i, qcol0 + p)),
        pl.BlockSpec((None, LANES, seq), lambda b, p, i: (b, p, 0)),
        pl.BlockSpec((None, LANES, seq), lambda b, p, i: (b, p, 0)),
    ]
    args = [r, k_arr, v_arr]
    if fox:
        in_specs += [pl.BlockSpec((1, SUBLANES, seq), lambda b, p, i: (b, 0, 0)),
                     pl.BlockSpec((tq, tq), lambda b, p, i: (0, 0))]
        args += [lf, tri]
        scratch = [pltpu.VMEM((SUBLANES, seq), F32)]
    else:
        scratch = [pltpu.VMEM((LANES, LANES), F32)]
    return pl.pallas_call(
        functools.partial(_attn_prompt_kernel, fox=fox, nblk=nq),
        grid=(batch, npair, nq),
        in_specs=in_specs,
        out_specs=pl.BlockSpec((tq, LANES), lambda b, p, i: (b * nq + i, p)),
        out_shape=jax.ShapeDtypeStruct((n, npair * LANES), F32),
        scratch_shapes=scratch,
        compiler_params=_cparams(("arbitrary", "arbitrary", "arbitrary")),
        name="fox_prompt" if fox else "moba_prompt",
    )(*args)


def _block_diag_q(q):
    nh = q.shape[1] // HEAD_DIM
    rows = nh * q.shape[0]
    qt = jnp.concatenate([q] * nh, axis=0)
    r_i = lax.broadcasted_iota(jnp.int32, (rows, q.shape[1]), 0)
    c_i = lax.broadcasted_iota(jnp.int32, (rows, q.shape[1]), 1)
    return jnp.where(r_i // q.shape[0] == c_i // HEAD_DIM, qt, 0.0)


def _diag_heads(o, t):
    nh = o.shape[1] // HEAD_DIM
    c_i = lax.broadcasted_iota(jnp.int32, (t, o.shape[1]), 1)
    out = jnp.zeros((t, o.shape[1]), F32)
    for h in range(nh):
        out = jnp.where(c_i // HEAD_DIM == h, o[h * t:(h + 1) * t], out)
    return out


def _expand_heads(c, t):
    nh = FOX_H
    return jnp.concatenate([jnp.broadcast_to(c[h:h + 1], (t, c.shape[1])) for h in range(nh)], axis=0)


def _softmax_update(state, s, v, dims, hi=False):
    m, l, acc = state
    m_new = jnp.maximum(m, jnp.max(s, axis=1, keepdims=True))
    alpha = jnp.exp(m - m_new)
    pexp = jnp.exp(s - m_new)
    l = l * alpha + jnp.sum(pexp, axis=1, keepdims=True)
    acc = acc * alpha + _dot(pexp, v, hi, dims)
    return m_new, l, acc


def _page_copies(pt_ref, srcs, bufs, sem_ref, layer, seq, chunk, slot, npg):
    copies = []
    for k in range(npg):
        page = pt_ref[seq, chunk * npg + k]
        for j, (src, buf) in enumerate(zip(srcs, bufs)):
            copies.append(pltpu.make_async_copy(src(layer, page), buf.at[slot, k], sem_ref.at[j, slot]))
    return copies


def _fox_decode_kernel(pt_ref, q_ref, kn_ref, vn_ref, lfn_ref, tri_ref, tri8_ref, ck_hbm, cv_hbm, clf_hbm,
                       o_ref, m_ref, l_ref, acc_ref, carry_ref, kcat_ref, vcat_ref, kbuf, vbuf, lfbuf, sem,
                       *, layer, npg, tsz):
    b = pl.program_id(0)
    c = pl.program_id(1)
    nb = pl.num_programs(0)
    nc = pl.num_programs(1)
    rows = FOX_H * tsz
    step = b * nc + c
    slot = step % 2
    srcs = (lambda l, p: ck_hbm.at[l, p], lambda l, p: cv_hbm.at[l, p],
            lambda l, p: clf_hbm.at[l, :, pl.ds(p, 1), :])
    bufs = (kbuf, vbuf, lfbuf)

    @pl.when(step == 0)
    def _():
        for cp in _page_copies(pt_ref, srcs, bufs, sem, layer, b, c, slot, npg):
            cp.start()

    @pl.when(step + 1 < nb * nc)
    def _():
        nxt = step + 1
        for cp in _page_copies(pt_ref, srcs, bufs, sem, layer, nxt // nc, nxt % nc, 1 - slot, npg):
            cp.start()

    @pl.when(c == 0)
    def _():
        m_ref[...] = jnp.full(m_ref.shape, NEG_BIG, F32)
        l_ref[...] = jnp.zeros(l_ref.shape, F32)
        acc_ref[...] = jnp.zeros(acc_ref.shape, F32)
        carry_ref[...] = jnp.zeros(carry_ref.shape, F32)

    qbd = _block_diag_q(q_ref[...] * (HEAD_DIM ** -0.5))
    for cp in _page_copies(pt_ref, srcs, bufs, sem, layer, b, c, slot, npg):
        cp.wait()
    state = (m_ref[:, 0:1], l_ref[:, 0:1], acc_ref[...])
    carry = carry_ref[:, 0:1]
    sub = lax.broadcasted_iota(jnp.int32, (SUBLANES, LANES), 0)
    lfs = []
    for k in range(npg):
        lf = jnp.zeros((SUBLANES, LANES), F32)
        for h in range(FOX_H):
            lf = jnp.where(sub == h, jnp.broadcast_to(lfbuf[slot, k, h], (SUBLANES, LANES)), lf)
        lfs.append(lf)
    cum = _dot_exact_rhs(jnp.concatenate(lfs, axis=0), tri_ref[...])
    cs = []
    for k in range(npg):
        ck = cum[k * SUBLANES:(k + 1) * SUBLANES] + carry
        carry = ck[:, LANES - 1:LANES]
        cs.append(_expand_heads(ck, tsz))
        kcat_ref[:, k * LANES:(k + 1) * LANES] = kbuf[slot, k].astype(BF16)
        vcat_ref[:, k * LANES:(k + 1) * LANES] = vbuf[slot, k].astype(BF16)
    last = (npg - 1) * LANES
    q_hi, q_lo = _split2(qbd)
    k_hi, k_lo = _split2(kbuf[slot, npg - 1])
    v_hi, v_lo = _split2(vbuf[slot, npg - 1])
    s = _mm(q_hi, kcat_ref[...])
    s = jnp.concatenate([s[:, 0:last], s[:, last:] + (_mm(q_hi, k_lo) + _mm(q_lo, k_hi))], axis=1)
    s = s - jnp.concatenate(cs, axis=1)
    m, l, acc = state
    m_new = jnp.maximum(m, jnp.max(s, axis=1, keepdims=True))
    alpha = jnp.exp(m - m_new)
    pexp = jnp.exp(s - m_new)
    p_hi, p_lo = _split2(pexp[:, last:])
    pv = _mm(pexp.astype(BF16), vcat_ref[...], _NT) + (_mm(p_hi, v_lo, _NT) + _mm(p_lo, v_hi, _NT))
    state = (m_new, l * alpha + jnp.sum(pexp, axis=1, keepdims=True), acc * alpha + pv)

    @pl.when(c < nc - 1)
    def _():
        m_ref[...] = jnp.broadcast_to(state[0], m_ref.shape)
        l_ref[...] = jnp.broadcast_to(state[1], l_ref.shape)
        acc_ref[...] = state[2]
        carry_ref[...] = jnp.broadcast_to(carry, carry_ref.shape)

    @pl.when(c == nc - 1)
    def _():
        cn = _dot_exact_rhs(lfn_ref[...], tri8_ref[...]) + carry
        s = _dot(qbd, kn_ref[...], True, _NT) - _expand_heads(cn, tsz)
        r_i = lax.broadcasted_iota(jnp.int32, (rows, tsz), 0)
        c_i = lax.broadcasted_iota(jnp.int32, (rows, tsz), 1)
        s = jnp.where(c_i <= r_i % tsz, s, -jnp.inf)
        _, l, acc = _softmax_update(state, s, vn_ref[...], _NN, True)
        o_ref[...] = _diag_heads(acc / l, tsz)


def _fox_decode(pt, zs, lfn, cache_k, cache_v, cache_lf, tri, tri8, layer, npg):
    nseq, npages = pt.shape
    tsz = zs.shape[0] // nseq
    nc = npages // npg
    rows = FOX_H * tsz
    any_spec = pl.BlockSpec(memory_space=pl.ANY)
    in_specs = [
        pl.BlockSpec((tsz, FOX_W), lambda b, c, pt: (b, C_FQ // FOX_W)),
        pl.BlockSpec((tsz, FOX_W), lambda b, c, pt: (b, C_FK // FOX_W)),
        pl.BlockSpec((tsz, FOX_W), lambda b, c, pt: (b, C_FV // FOX_W)),
        pl.BlockSpec((None, SUBLANES, tsz), lambda b, c, pt: (b, 0, 0)),
        pl.BlockSpec((LANES, LANES), lambda b, c, pt: (0, 0)),
        pl.BlockSpec((tsz, tsz), lambda b, c, pt: (0, 0)),
        any_spec, any_spec, any_spec,
    ]
    grid_spec = pltpu.PrefetchScalarGridSpec(
        num_scalar_prefetch=1,
        grid=(nseq, nc),
        in_specs=in_specs,
        out_specs=pl.BlockSpec((tsz, FOX_W), lambda b, c, pt: (b, 0)),
        scratch_shapes=[pltpu.VMEM((rows, LANES), F32), pltpu.VMEM((rows, LANES), F32),
                        pltpu.VMEM((rows, FOX_W), F32), pltpu.VMEM((SUBLANES, LANES), F32),
                        pltpu.VMEM((FOX_W, npg * LANES), BF16), pltpu.VMEM((FOX_W, npg * LANES), BF16),
                        pltpu.VMEM((2, npg, FOX_W, LANES), F32), pltpu.VMEM((2, npg, FOX_W, LANES), F32),
                        pltpu.VMEM((2, npg, FOX_H, 1, LANES), F32), pltpu.SemaphoreType.DMA((3, 2))],
    )
    return pl.pallas_call(
        functools.partial(_fox_decode_kernel, layer=layer, npg=npg, tsz=tsz),
        grid_spec=grid_spec,
        out_shape=jax.ShapeDtypeStruct((nseq * tsz, FOX_W), F32),
        compiler_params=_cparams(("arbitrary", "arbitrary")),
        name="fox_decode",
    )(pt, zs, zs, zs, lfn, tri, tri8, cache_k, cache_v, cache_lf)


def _moba_decode_kernel(pt_ref, q_ref, kn_ref, vn_ref, ck_hbm, cv_hbm, o_ref, s_ref, g_ref, sel_ref, m_ref, l_ref,
                        acc_ref, cat_ref, pbuf, sem, *, layer, npg, tsz):
    b = pl.program_id(0)
    ph = pl.program_id(1)
    c = pl.program_id(2)
    nb = pl.num_programs(0)
    nc = pl.num_programs(2)
    rows = MOBA_H * tsz
    ppb = MOBA_BLOCK // LANES
    nblk = npg // ppb
    step = (b * 2 + ph) * nc + c
    slot = step % 2
    bufs = (pbuf,)
    k_src = (lambda l, p: ck_hbm.at[l, p],)
    v_src = (lambda l, p: cv_hbm.at[l, p],)

    def copies(srcs, st, sl):
        return _page_copies(pt_ref, srcs, bufs, sem, layer, st // (2 * nc), st % nc, sl, npg)

    @pl.when(step == 0)
    def _():
        for cp in copies(k_src, step, slot):
            cp.start()

    nxt = step + 1
    nxt_ph = (nxt // nc) % 2

    @pl.when(jnp.logical_and(nxt < nb * 2 * nc, nxt_ph == 0))
    def _():
        for cp in copies(k_src, nxt, 1 - slot):
            cp.start()

    @pl.when(jnp.logical_and(nxt < nb * 2 * nc, nxt_ph == 1))
    def _():
        for cp in copies(v_src, nxt, 1 - slot):
            cp.start()

    qbd = _block_diag_q(q_ref[...] * (HEAD_DIM ** -0.5)).astype(BF16)
    lane = lax.broadcasted_iota(jnp.int32, (rows, LANES), 1)
    off = pl.multiple_of(c * (npg * LANES), npg * LANES)

    @pl.when(ph == 0)
    def _():
        for cp in copies(k_src, step, slot):
            cp.wait()

        @pl.when(c == 0)
        def _():
            g_ref[...] = jnp.full(g_ref.shape, -jnp.inf, F32)
        g = g_ref[...]
        for k in range(npg):
            cat_ref[:, k * LANES:(k + 1) * LANES] = pbuf[slot, k].astype(BF16)
        s = _mm(qbd, cat_ref[...])
        s_ref[:, pl.ds(off, npg * LANES)] = s
        for jb in range(nblk):
            tot = jnp.sum(s[:, jb * MOBA_BLOCK:(jb + 1) * MOBA_BLOCK], axis=1, keepdims=True)
            g = jnp.where(lane == c * nblk + jb, tot * (1.0 / MOBA_BLOCK), g)
        g_ref[...] = g

    @pl.when(ph == 1)
    def _():
        for cp in copies(v_src, step, slot):
            cp.wait()

        @pl.when(c == 0)
        def _():
            sel_ref[...] = _top_select(g_ref[...], MOBA_TOPK)
            s = _mm(qbd, kn_ref[...].astype(BF16), _NT)
            r_i = lax.broadcasted_iota(jnp.int32, (rows, tsz), 0)
            c_i = lax.broadcasted_iota(jnp.int32, (rows, tsz), 1)
            s = jnp.where(c_i <= r_i % tsz, s, -jnp.inf)
            init = (jnp.full((rows, 1), NEG_BIG, F32), jnp.zeros((rows, 1), F32),
                    jnp.zeros((rows, MOBA_W), F32))
            m, l, acc = _softmax_update(init, s, vn_ref[...], _NN)
            m_ref[...] = jnp.broadcast_to(m, m_ref.shape)
            l_ref[...] = jnp.broadcast_to(l, l_ref.shape)
            acc_ref[...] = acc

        m, l, acc = m_ref[:, 0:1], l_ref[:, 0:1], acc_ref[...]
        sel = sel_ref[...]
        sc = s_ref[:, pl.ds(off, npg * LANES)]
        ss = []
        for jb in range(nblk):
            on = jnp.max(jnp.where(lane == c * nblk + jb, sel, 0.0), axis=1, keepdims=True)
            ss.append(jnp.where(on > 0.0, sc[:, jb * MOBA_BLOCK:(jb + 1) * MOBA_BLOCK], -jnp.inf))
        for k in range(npg):
            cat_ref[:, k * LANES:(k + 1) * LANES] = pbuf[slot, k].astype(BF16)
        s = jnp.concatenate(ss, axis=1)
        m_new = jnp.maximum(m, jnp.max(s, axis=1, keepdims=True))
        alpha = jnp.exp(m - m_new)
        pexp = jnp.exp(s - m_new)
        pv = _mm(pexp.astype(BF16), cat_ref[...], _NT)
        l_new = l * alpha + jnp.sum(pexp, axis=1, keepdims=True)
        acc_new = acc * alpha + pv
        m_ref[...] = jnp.broadcast_to(m_new, m_ref.shape)
        l_ref[...] = jnp.broadcast_to(l_new, l_ref.shape)
        acc_ref[...] = acc_new

        @pl.when(c == nc - 1)
        def _():
            o_ref[...] = _diag_heads(acc_new / l_new, tsz)


def _moba_decode(pt, zs, cache_k, cache_v, layer, npg):
    nseq, npages = pt.shape
    tsz = zs.shape[0] // nseq
    nc = npages // npg
    rows = MOBA_H * tsz
    any_spec = pl.BlockSpec(memory_space=pl.ANY)
    in_specs = [
        pl.BlockSpec((tsz, MOBA_W), lambda b, ph, c, pt: (b, C_MQ // MOBA_W)),
        pl.BlockSpec((tsz, MOBA_W), lambda b, ph, c, pt: (b, C_MK // MOBA_W)),
        pl.BlockSpec((tsz, MOBA_W), lambda b, ph, c, pt: (b, C_MV // MOBA_W)),
        any_spec, any_spec,
    ]
    grid_spec = pltpu.PrefetchScalarGridSpec(
        num_scalar_prefetch=1,
        grid=(nseq, 2, nc),
        in_specs=in_specs,
        out_specs=pl.BlockSpec((tsz, MOBA_W), lambda b, ph, c, pt: (b, 0)),
        scratch_shapes=[pltpu.VMEM((rows, npages * LANES), F32), pltpu.VMEM((rows, LANES), F32),
                        pltpu.VMEM((rows, LANES), F32), pltpu.VMEM((rows, LANES), F32),
                        pltpu.VMEM((rows, LANES), F32), pltpu.VMEM((rows, MOBA_W), F32),
                        pltpu.VMEM((MOBA_W, npg * LANES), BF16),
                        pltpu.VMEM((2, npg, MOBA_W, LANES), F32), pltpu.SemaphoreType.DMA((1, 2))],
    )
    return pl.pallas_call(
        functools.partial(_moba_decode_kernel, layer=layer, npg=npg, tsz=tsz),
        grid_spec=grid_spec,
        out_shape=jax.ShapeDtypeStruct((nseq * tsz, MOBA_W), F32),
        compiler_params=_cparams(("arbitrary", "arbitrary", "arbitrary")),
        name="moba_decode",
    )(pt, zs, zs, zs, cache_k, cache_v)


def _outproj_kernel(*refs, alpha, sample, hi, nts):
    if sample:
        x_ref, cb_ref, cc_ref, cx_ref, h0_ref, h1_ref, mo_ref, fo_ref, wo_ref, cw_ref, g_ref, b_ref, \
            y_ref, u_ref = refs
    else:
        x_ref, cb_ref, cc_ref, cx_ref, cch_ref, cxh_ref, mo_ref, fo_ref, wo_ref, cw_ref, g_ref, b_ref, \
            y_ref, u_ref = refs
    u = cc_ref[...] * cx_ref[...]
    tm = u.shape[0]
    row = lax.broadcasted_iota(jnp.int32, u.shape, 0)
    r1 = pltpu.roll(u, 1, 0)
    r2 = pltpu.roll(u, 2, 0)
    if sample:
        t = row % SUBLANES
        s1 = jnp.where(t == 0, h1_ref[...], r1)
        s2 = jnp.where(t == 0, h0_ref[...], jnp.where(t == 1, h1_ref[...], r2))
        u_ref[...] = u
    else:
        first = pl.program_id(0) % nts == 0
        uh = jnp.where(first, 0.0, cch_ref[...] * cxh_ref[...])
        p1 = jnp.broadcast_to(uh[SUBLANES - 1:SUBLANES], u.shape)
        p2 = jnp.broadcast_to(uh[SUBLANES - 2:SUBLANES - 1], u.shape)
        s1 = jnp.where(row == 0, p1, r1)
        s2 = jnp.where(row == 0, p2, jnp.where(row == 1, p1, r2))
        u_ref[...] = u[tm - SUBLANES:tm]
    cw = cw_ref[...]
    conv = cb_ref[...] * (s2 * cw[0:1] + s1 * cw[1:2] + u * cw[2:3])
    mixed = (_dot(conv, wo_ref[0:CONV_W, :], hi)
             + _dot(mo_ref[...], wo_ref[CONV_W:CONV_W + MOBA_W, :], hi)
             + _dot(fo_ref[...], wo_ref[CONV_W + MOBA_W:, :], hi))
    y_ref[...] = _layer_norm(alpha * x_ref[...] + mixed, g_ref[...], b_ref[...])


def _outproj(x, r, ccol, mo, fo, wo, cw, g, b, alpha, tm, seq, hist=None):
    n, d = x.shape
    sample = hist is not None
    nts = 1 if sample else seq // tm
    hb = tm // SUBLANES
    col = lambda k: pl.BlockSpec((tm, CONV_W), lambda i: (i, ccol + k))
    if sample:
        extra = [pl.BlockSpec((tm, CONV_W), lambda i: (i, 0)), pl.BlockSpec((tm, CONV_W), lambda i: (i, 0))]
        extra_args = list(hist)
        u_shape, u_spec = (n, CONV_W), pl.BlockSpec((tm, CONV_W), lambda i: (i, 0))
    else:
        halo = lambda k: pl.BlockSpec((SUBLANES, CONV_W), lambda i: (jnp.maximum(i * hb - 1, 0), ccol + k))
        extra = [halo(1), halo(2)]
        extra_args = [r, r]
        u_shape, u_spec = (n // tm * SUBLANES, CONV_W), pl.BlockSpec((SUBLANES, CONV_W), lambda i: (i, 0))
    return pl.pallas_call(
        functools.partial(_outproj_kernel, alpha=alpha, sample=sample, hi=sample, nts=nts),
        grid=(n // tm,),
        in_specs=[pl.BlockSpec((tm, d), lambda i: (i, 0)), col(0), col(1), col(2)] + extra + [
            pl.BlockSpec((tm, MOBA_W), lambda i: (i, 0)),
            pl.BlockSpec((tm, FOX_W), lambda i: (i, 0)),
            pl.BlockSpec(wo.shape, lambda i: (0, 0)),
            pl.BlockSpec(cw.shape, lambda i: (0, 0)),
            pl.BlockSpec((1, d), lambda i: (0, 0)),
            pl.BlockSpec((1, d), lambda i: (0, 0)),
        ],
        out_specs=[pl.BlockSpec((tm, d), lambda i: (i, 0)), u_spec],
        out_shape=[jax.ShapeDtypeStruct((n, d), F32), jax.ShapeDtypeStruct(u_shape, F32)],
        compiler_params=_cparams(("arbitrary",)),
        name="outproj_sample" if sample else "outproj_prompt",
    )(x, r, r, r, *extra_args, mo, fo, wo, cw, g, b)


def _ffn_kernel(src_ref, eid_ref, ns_ref, *refs, g, hi):
    x_refs = refs[:g]
    wg_ref, wu_ref, wd_ref, o_ref, acc_ref = refs[g:]
    i = pl.program_id(0)
    f = pl.program_id(1)
    last = f == pl.num_programs(1) - 1
    live = i < ns_ref[0]

    @pl.when(live)
    def _():
        x = x_refs[0][...] if g == 1 else jnp.concatenate([r[...] for r in x_refs], axis=0)
        h = _silu(_dot(x, wg_ref[...], hi)) * _dot(x, wu_ref[...], hi)
        part = _dot(h, wd_ref[...], hi)

        @pl.when(f == 0)
        def _():
            acc_ref[...] = part

        @pl.when(f > 0)
        def _():
            acc_ref[...] = acc_ref[...] + part

        @pl.when(last)
        def _():
            o_ref[...] = acc_ref[...].astype(o_ref.dtype)

    @pl.when(jnp.logical_and(jnp.logical_not(live), last))
    def _():
        o_ref[...] = jnp.zeros(o_ref.shape, o_ref.dtype)


def _ffn(x, src, eid, nsteps, wg, wu, wd, rows, g, tf, hi, out_dtype):
    d = x.shape[1]
    ff = wg.shape[2]
    ns = eid.shape[0]

    def x_spec(j):
        return pl.BlockSpec((rows, d), lambda i, f, src, eid, n: (src[i * g + j], 0))

    grid_spec = pltpu.PrefetchScalarGridSpec(
        num_scalar_prefetch=3,
        grid=(ns, ff // tf),
        in_specs=[x_spec(j) for j in range(g)] + [
            pl.BlockSpec((None, d, tf), lambda i, f, src, eid, n: (eid[i], 0, f)),
            pl.BlockSpec((None, d, tf), lambda i, f, src, eid, n: (eid[i], 0, f)),
            pl.BlockSpec((None, tf, d), lambda i, f, src, eid, n: (eid[i], f, 0)),
        ],
        out_specs=pl.BlockSpec((g * rows, d), lambda i, f, src, eid, n: (i, 0)),
        scratch_shapes=[pltpu.VMEM((g * rows, d), F32)],
    )
    return pl.pallas_call(
        functools.partial(_ffn_kernel, g=g, hi=hi),
        grid_spec=grid_spec,
        out_shape=jax.ShapeDtypeStruct((ns * g * rows, d), out_dtype),
        compiler_params=_cparams(("arbitrary", "arbitrary")),
        name="ffn_hi" if hi else "ffn",
    )(src, eid, nsteps, *([x] * g), wg, wu, wd)


def _router_kernel(x_ref, w_ref, cwt_ref, cnt_ref, *, n_exp):
    logits = _dot(x_ref[...], w_ref[...], True)
    lane = lax.broadcasted_iota(jnp.int32, logits.shape, 1)
    logits = jnp.where(lane < n_exp, logits, -jnp.inf)
    m = jnp.max(logits, axis=1, keepdims=True)
    e = jnp.exp(logits - m)
    probs = e / jnp.sum(e, axis=1, keepdims=True)
    sel = _top_select(jnp.where(lane < n_exp, probs, -jnp.inf), TOP_K)
    top = probs * sel
    cw = top / jnp.sum(top, axis=1, keepdims=True)
    cwt_ref[...] = jnp.transpose(cw)[0:SUBLANES]
    cnt_ref[...] = jnp.broadcast_to(jnp.sum(sel, axis=0, keepdims=True), cnt_ref.shape)


def _router(x, w, n_exp, tm):
    n, d = x.shape
    return pl.pallas_call(
        functools.partial(_router_kernel, n_exp=n_exp),
        grid=(n // tm,),
        in_specs=[pl.BlockSpec((tm, d), lambda i: (i, 0)), pl.BlockSpec((d, LANES), lambda i: (0, 0))],
        out_specs=[pl.BlockSpec((SUBLANES, tm), lambda i: (0, i)),
                   pl.BlockSpec((SUBLANES, LANES), lambda i: (i, 0))],
        out_shape=[jax.ShapeDtypeStruct((SUBLANES, n), F32),
                   jax.ShapeDtypeStruct((n // tm * SUBLANES, LANES), F32)],
        compiler_params=_cparams(("arbitrary",)),
        name="router",
    )(x, w)


def _slot_rows(cwt_ref, tri_ref):
    on = cwt_ref[...] > 0.0
    cum = _mm(jnp.where(on, 1.0, 0.0).astype(BF16), tri_ref[...])
    return jnp.where(on, cum - 1.0, -1.0)


def _pack_kernel(cnt_ref, x_ref, cwt_ref, tri_ref, xc_ref, xb_ref, pos_ref):
    t = pl.program_id(0)
    e = pl.program_id(1)
    tm = x_ref.shape[0]

    @pl.when(e == 0)
    def _():
        xb_ref[...] = x_ref[...].astype(BF16)
        pos_ref[...] = _slot_rows(cwt_ref, tri_ref)

    row = pos_ref[pl.ds(e, 1), :]
    sub = lax.broadcasted_iota(jnp.int32, (ROUTE_CHUNK, tm), 0).astype(F32)
    for k in range(tm // ROUTE_CHUNK):
        lo = k * ROUTE_CHUNK

        @pl.when(cnt_ref[t, e] > lo)
        def _():
            sel = jnp.where(row - float(lo) == sub, 1.0, 0.0).astype(BF16)
            xc_ref[lo:lo + ROUTE_CHUNK, :] = _mm(sel, xb_ref[...]).astype(BF16)

        @pl.when(cnt_ref[t, e] <= lo)
        def _():
            xc_ref[lo:lo + ROUTE_CHUNK, :] = jnp.zeros((ROUTE_CHUNK, xc_ref.shape[1]), BF16)


def _pack(x, cwt, counts, tri, tm):
    n, d = x.shape
    n_exp = counts.shape[1]
    grid_spec = pltpu.PrefetchScalarGridSpec(
        num_scalar_prefetch=1,
        grid=(n // tm, n_exp),
        in_specs=[
            pl.BlockSpec((tm, d), lambda t, e, c: (t, 0)),
            pl.BlockSpec((SUBLANES, tm), lambda t, e, c: (0, t)),
            pl.BlockSpec((tm, tm), lambda t, e, c: (0, 0)),
        ],
        out_specs=pl.BlockSpec((tm, d), lambda t, e, c: (t * n_exp + e, 0)),
        scratch_shapes=[pltpu.VMEM((tm, d), BF16), pltpu.VMEM((SUBLANES, tm), F32)],
    )
    return pl.pallas_call(
        _pack_kernel,
        grid_spec=grid_spec,
        out_shape=jax.ShapeDtypeStruct((n * n_exp, d), BF16),
        compiler_params=_cparams(("arbitrary", "arbitrary")),
        name="moe_pack",
    )(counts, x, cwt, tri)


def _unpack_kernel(spos_ref, se_ref, sk_ref, x_ref, cwt_ref, tri_ref, g_ref, b_ref, *rest, alpha, nslot):
    y_refs = rest[:nslot]
    o_ref, pos_ref = rest[nslot:]
    t = pl.program_id(0)
    tm = x_ref.shape[0]
    pos_ref[...] = _slot_rows(cwt_ref, tri_ref)
    sub = lax.broadcasted_iota(jnp.int32, (ROUTE_CHUNK, tm), 0).astype(F32)
    acc = alpha * x_ref[...]
    for s in range(nslot):
        e = se_ref[t, s]
        ee = jnp.maximum(e, 0)
        lo = (sk_ref[t, s] * ROUTE_CHUNK).astype(F32)
        row = pos_ref[pl.ds(ee, 1), :]
        w = jnp.where(e >= 0, cwt_ref[pl.ds(ee, 1), :], 0.0)
        sel = jnp.where(row - lo == sub, w, 0.0)
        acc = acc + _mm(jnp.transpose(sel).astype(BF16), y_refs[s][...])
    o_ref[...] = _layer_norm(acc, g_ref[...], b_ref[...])


def _unpack(x, y, cwt, tri, slot_pos, slot_e, slot_k, g, b, alpha, tm):
    n, d = x.shape
    nslot = slot_pos.shape[1]

    def y_spec(s):
        return pl.BlockSpec((ROUTE_CHUNK, d), lambda t, sp, se, sk: (sp[t, s], 0))

    grid_spec = pltpu.PrefetchScalarGridSpec(
        num_scalar_prefetch=3,
        grid=(n // tm,),
        in_specs=[
            pl.BlockSpec((tm, d), lambda t, sp, se, sk: (t, 0)),
            pl.BlockSpec((SUBLANES, tm), lambda t, sp, se, sk: (0, t)),
            pl.BlockSpec((tm, tm), lambda t, sp, se, sk: (0, 0)),
            pl.BlockSpec((1, d), lambda t, sp, se, sk: (0, 0)),
            pl.BlockSpec((1, d), lambda t, sp, se, sk: (0, 0)),
        ] + [y_spec(s) for s in range(nslot)],
        out_specs=pl.BlockSpec((tm, d), lambda t, sp, se, sk: (t, 0)),
        scratch_shapes=[pltpu.VMEM((SUBLANES, tm), F32)],
    )
    return pl.pallas_call(
        functools.partial(_unpack_kernel, alpha=alpha, nslot=nslot),
        grid_spec=grid_spec,
        out_shape=jax.ShapeDtypeStruct((n, d), F32),
        compiler_params=_cparams(("arbitrary",)),
        name="moe_unpack",
    )(slot_pos, slot_e, slot_k, x, cwt, tri, g, b, *([y] * nslot))


def _residual_ln_kernel(x_ref, y_ref, g_ref, b_ref, o_ref, *, alpha):
    o_ref[...] = _layer_norm(alpha * x_ref[...] + y_ref[...], g_ref[...], b_ref[...])


def _residual_ln(x, y, g, b, alpha, tm):
    n, d = x.shape
    row = pl.BlockSpec((tm, d), lambda i: (i, 0))
    vec = pl.BlockSpec((1, d), lambda i: (0, 0))
    return pl.pallas_call(
        functools.partial(_residual_ln_kernel, alpha=alpha),
        grid=(n // tm,),
        in_specs=[row, row, vec, vec],
        out_specs=row,
        out_shape=jax.ShapeDtypeStruct((n, d), F32),
        compiler_params=_cparams(("arbitrary",)),
        name="residual_ln",
    )(x, y, g, b)


def _rope_tables(pos):
    inv = jnp.power(ROPE_THETA, -jnp.arange(0, HEAD_DIM, 2, dtype=F32) / HEAD_DIM)
    ang = pos.astype(F32)[:, None] * inv[None, :]
    return jnp.cos(ang), jnp.sin(ang)


def _row_tables(cos, sin, nh):
    cq = jnp.tile(jnp.concatenate([cos, cos], axis=1), (1, nh))
    sq = jnp.tile(jnp.concatenate([-sin, sin], axis=1), (1, nh))
    return cq, sq


def _upper_ones(n):
    return (jnp.arange(n)[:, None] <= jnp.arange(n)[None, :]).astype(BF16)


def _route_plan(counts, tm, group):
    nt, n_exp = counts.shape
    kmax = tm // ROUTE_CHUNK
    nslot = n_exp + TOP_K * kmax
    nch = (counts + ROUTE_CHUNK - 1) // ROUTE_CHUNK
    tot = jnp.sum(nch, axis=0)
    tot_pad = (tot + group - 1) // group * group
    off = jnp.cumsum(tot_pad) - tot_pad
    before = jnp.cumsum(nch, axis=0) - nch
    k = jnp.arange(kmax, dtype=jnp.int32)
    lpos = off[None, :, None] + before[:, :, None] + k[None, None, :]
    valid = k[None, None, :] < nch[:, :, None]
    t_i = jnp.arange(nt, dtype=jnp.int32)[:, None, None]
    e_i = jnp.arange(n_exp, dtype=jnp.int32)[None, :, None]
    src = (t_i * n_exp + e_i) * kmax + k[None, None, :]
    lmax = (nt * nslot + n_exp * (group - 1) + group - 1) // group * group
    src_list = jnp.zeros((lmax,), jnp.int32).at[jnp.where(valid, lpos, lmax).reshape(-1)].set(
        src.reshape(-1).astype(jnp.int32), mode="drop")
    step0 = jnp.arange(lmax // group, dtype=jnp.int32) * group
    ends = jnp.cumsum(tot_pad)
    eid = jnp.minimum(jnp.sum(step0[:, None] >= ends[None, :], axis=1), n_exp - 1).astype(jnp.int32)
    nsteps = (jnp.sum(tot_pad) // group).astype(jnp.int32).reshape(1)
    sidx = (jnp.cumsum(nch, axis=1) - nch)[:, :, None] + k[None, None, :]
    sidx = jnp.where(valid, sidx, nslot)
    rows = jnp.broadcast_to(t_i, sidx.shape).reshape(-1)

    def scatter(vals, fill):
        base = jnp.full((nt, nslot), fill, jnp.int32)
        return base.at[rows, sidx.reshape(-1)].set(jnp.broadcast_to(vals, sidx.shape).reshape(-1).astype(jnp.int32),
                                                   mode="drop")
    return src_list, eid, nsteps, scatter(lpos, 0), scatter(e_i, -1), scatter(k[None, None, :], 0)


def _channel_mixer(x, l, alpha, g, b, dense, moe, tm, hi):
    n = x.shape[0]
    nt = n // tm
    i = l // 2
    if l % 2 == 0:
        wg, wu, wd = (w[i:i + 1] for w in dense)
        tm = min(ROUTE_TILE, n)
        nt = n // tm
        y = _ffn(x, jnp.arange(nt, dtype=jnp.int32), jnp.zeros((nt,), jnp.int32), jnp.full((1,), nt, jnp.int32),
                 wg, wu, wd, tm, 1, 512 if hi else FFN_TILE, hi, F32)
        return _residual_ln(x, y, g, b, alpha, tm)
    router_w, wg, wu, wd = moe
    n_exp = wg.shape[1]
    tm = min(ROUTE_TILE, n)
    nt = n // tm
    rw = jnp.pad(router_w[i], ((0, 0), (0, LANES - n_exp)))
    cwt, cnt = _router(x, rw, n_exp, tm)
    counts = cnt.reshape(nt, SUBLANES, LANES)[:, 0, 0:n_exp].astype(jnp.int32)
    tri = _upper_ones(tm)
    xc = _pack(x, cwt, counts, tri, tm)
    group = max(1, min(ROUTE_GROUP, nt * TOP_K * tm // (n_exp * ROUTE_CHUNK)))
    src, eid, nsteps, slot_pos, slot_e, slot_k = _route_plan(counts, tm, group)
    y = _ffn(xc, src, eid, nsteps, wg[i], wu[i], wd[i], ROUTE_CHUNK, group, FFN_TILE, False, BF16)
    return _unpack(x, y, cwt, tri, slot_pos, slot_e, slot_k, g, b, alpha, tm)


def kernel(x_prompt, x_sample, cache_moba_k, cache_moba_v, cache_fox_k, cache_fox_v, cache_fox_logf, state_conv,
           page_table, w_in, conv_w, fox_fbias, w_out, ln_mix_g, ln_mix_b, ln_ffn_g, ln_ffn_b, dense_w_gate,
           dense_w_up, dense_w_down, router_w, moe_w_gate, moe_w_up, moe_w_down):
    depth = w_in.shape[0]
    bp, seq, d = x_prompt.shape
    bs, tsz, _ = x_sample.shape
    n_pages = page_table.shape[1]
    page = cache_moba_k.shape[2]
    past = n_pages * page
    alpha = (2.0 * depth) ** 0.25
    assert page == LANES and tsz == SUBLANES and MOBA_BLOCK % page == 0 and past % MOBA_BLOCK == 0
    assert seq % MOBA_BLOCK == 0

    tm_p = min(512, seq)
    npg = 32
    hp = x_prompt.reshape(bp * seq, d)
    hs = x_sample.reshape(bs * tsz, d)

    cos_p, sin_p = _rope_tables(jnp.arange(seq))
    cq_p, sq_p = _row_tables(cos_p, sin_p, MOBA_H)
    ct_p, st_p = cos_p.T, sin_p.T
    cos_s, sin_s = _rope_tables(past + jnp.arange(tsz))
    cq_s, sq_s = _row_tables(jnp.tile(cos_s, (bs, 1)), jnp.tile(sin_s, (bs, 1)), MOBA_H)

    tri_blk, tri_pg, tri_t = _upper_ones(MOBA_BLOCK), _upper_ones(LANES), _upper_ones(tsz)

    def pages_t(c, w):
        return jnp.transpose(c, (0, 1, 3, 4, 2)).reshape(c.shape[0], c.shape[1], w, page)
    cmk, cmv = pages_t(cache_moba_k, MOBA_W), pages_t(cache_moba_v, MOBA_W)
    cfk, cfv = pages_t(cache_fox_k, FOX_W), pages_t(cache_fox_v, FOX_W)
    clf = jnp.transpose(cache_fox_logf, (0, 3, 1, 2))

    dense_b = tuple(w.astype(BF16) for w in (dense_w_gate, dense_w_up, dense_w_down))
    dense_f = (dense_w_gate, dense_w_up, dense_w_down)
    moe_b = (router_w,) + tuple(w.astype(BF16) for w in (moe_w_gate, moe_w_up, moe_w_down))

    rows_p, rows_s = [], []
    for l in range(depth):
        wl = w_in[l]
        fbias = jnp.pad(fox_fbias[l], (0, SUBLANES - FOX_H))
        wo = w_out[l]
        cwl = jnp.pad(conv_w[l], ((0, SUBLANES - CONV_K), (0, 0)))
        g1, b1 = ln_mix_g[l][None], ln_mix_b[l][None]
        g2, b2 = ln_ffn_g[l][None], ln_ffn_b[l][None]

        w_row = jnp.concatenate([wl[:, 0:C_MK], wl[:, C_FQ:C_FK]], axis=1).astype(BF16)
        w_col = jnp.concatenate([wl[:, C_MK:C_FQ], wl[:, C_FK:C_FG],
                                 jnp.pad(wl[:, C_FG:], ((0, 0), (0, SUBLANES - FOX_H)))], axis=1).T.astype(BF16)
        r, *kv, lf = _inproj_prompt(hp, w_row, w_col, fbias[:, None], cq_p, sq_p, ct_p, st_p, bp, seq, tm_p)
        mo = _attn_prompt(r, kv[0], kv[1], None, None, bp, seq, False)
        fo = _attn_prompt(r, kv[2], kv[3], lf, tri_blk, bp, seq, True)
        hp1, utail = _outproj(hp, r, 0, mo, fo, wo.astype(BF16), cwl, g1, b1, alpha, tm_p, seq)
        hp = _channel_mixer(hp1, l, alpha, g2, b2, dense_b, moe_b, tm_p, False)
        nts = seq // tm_p
        conv_new = utail.reshape(bp, nts, SUBLANES, CONV_W)[:, nts - 1, SUBLANES - (CONV_K - 1):]
        heads_p = [jnp.transpose(a.reshape(bp, MOBA_H, HEAD_DIM, seq), (0, 3, 1, 2)) for a in kv]
        rows_p.append((*heads_p, jnp.transpose(lf[:, 0:FOX_H], (0, 2, 1)), conv_new))

        w_s = jnp.pad(wl, ((0, 0), (0, C_FG + MOBA_W - wl.shape[1])))
        fb_row = jnp.pad(fox_fbias[l], (0, MOBA_W - FOX_H))[None]
        zs = _inproj_sample(hs, w_s, fb_row, cq_s, sq_s)
        lfs = zs[:, C_FG:C_FG + LANES]
        lfn = jnp.transpose(lfs[:, 0:SUBLANES].reshape(bs, tsz, SUBLANES), (0, 2, 1))
        mo_s = _moba_decode(page_table, zs, cmk, cmv, l, npg)
        fo_s = _fox_decode(page_table, zs, lfn, cfk, cfv, clf, tri_pg, tri_t, l, npg)
        st = state_conv[l]
        hist = (jnp.repeat(st[:, 0], tsz, axis=0), jnp.repeat(st[:, 1], tsz, axis=0))
        hs1, us = _outproj(hs, zs, 0, mo_s, fo_s, wo, cwl, g1, b1, alpha, bs * tsz, tsz, hist)
        hs = _channel_mixer(hs1, l, alpha, g2, b2, dense_f, moe_b, bs * tsz, True)
        u_ext = jnp.concatenate([st, us.reshape(bs, tsz, CONV_W)], axis=1)
        heads = lambda c0: zs[:, c0:c0 + MOBA_W].reshape(bs, tsz, MOBA_H, HEAD_DIM)
        rows_s.append((heads(C_MK), heads(C_MV), heads(C_FK), heads(C_FV),
                       lfs[:, 0:FOX_H].reshape(bs, tsz, FOX_H), u_ext[:, tsz:]))

    stack = lambda rows, k: jnp.stack([r[k] for r in rows])
    return (hp.reshape(bp, seq, d), hs.reshape(bs, tsz, d),
            *(stack(rows_p, k) for k in range(6)), *(stack(rows_s, k) for k in range(6)))
```
